```python
import math
import jax, jax.numpy as jnp
from jax import lax
import numpy as np


D_MODEL = 1024
BATCH = 8
SEQ = 4096
DEPTH = 4

RW_HEAD_DIM = 64
RW_WIDTH = D_MODEL // 2
RW_HEADS = RW_WIDTH // RW_HEAD_DIM
RW_DECAY_LORA = 64
RW_A_LORA = 64
RW_GATE_LORA = 160
RW_LN_EPS = 64e-5
GLA_HEADS = 4
GLA_VW = D_MODEL // 2
GLA_DV = GLA_VW // GLA_HEADS
GLA_DK = GLA_DV // 2
GLA_KW = GLA_HEADS * GLA_DK
GLA_GATE_LORA = 16
GLA_GATE_NORMALIZER = 16.0
GLA_CHUNK = 64
GLA_NORM_EPS = 1e-5
AB_IN = 3 * RW_WIDTH + 2 * GLA_KW + 2 * GLA_VW
AB_OUT = RW_WIDTH + GLA_VW
DA_QK_DIM = 64
DA_V_DIM = 2 * DA_QK_DIM
DA_HEADS = D_MODEL // DA_V_DIM
DA_QK_WIDTH = DA_HEADS * 2 * DA_QK_DIM
DA_V_WIDTH = DA_HEADS * DA_V_DIM
DA_Q_BLOCK = 128
DA_SUBLN_EPS = 1e-5
NEG_INF = -1e30
REL_BUCKETS = 32
REL_MAX_DIST = 128
FFN_HIDDEN = ((-(-8 * D_MODEL // 3) + 255) // 256) * 256
NORM_EPS = 1e-6
N_EVEN = (DEPTH + 1) // 2
N_ODD = DEPTH // 2

kernel_name = "hybrid_rwkv7_gla_diffattn_swiglu"


def rms_norm(x, g, eps=NORM_EPS):
    xf = x.astype(jnp.float32)
    y = xf * lax.rsqrt(jnp.mean(xf * xf, axis=-1, keepdims=True) + eps)
    return (y * g.astype(jnp.float32)).astype(x.dtype)


def token_shift(t):
    return jnp.pad(t, ((0, 0), (1, 0), (0, 0)))[:, :-1]


def swiglu(h, w_gate, w_up, w_down):
    return (jax.nn.silu(h @ w_gate) * (h @ w_up)) @ w_down


def t5_causal_buckets(dist):
    max_exact = REL_BUCKETS // 2
    ratio = jnp.maximum(dist, max_exact).astype(jnp.float32) / max_exact
    large = max_exact + (jnp.log(ratio) / math.log(REL_MAX_DIST / max_exact)
                         * (REL_BUCKETS - max_exact)).astype(jnp.int32)
    large = jnp.minimum(large, REL_BUCKETS - 1)
    return jnp.where(dist < max_exact, dist, large)


def diff_lambda_init(layer):
    return 0.8 - 0.6 * math.exp(-0.3 * layer)


def rwkv7_recurrence(r, w, k, v, a, b):
    Bsz, T, H, N = r.shape

    def step(S, inp):
        r_t, w_t, k_t, v_t, a_t, b_t = inp
        sa = jnp.einsum('bhvk,bhk->bhv', S, a_t)
        S = S * w_t[:, :, None, :] + sa[..., None] * b_t[:, :, None, :] + v_t[..., None] * k_t[:, :, None, :]
        return S, jnp.einsum('bhvk,bhk->bhv', S, r_t)

    xs = tuple(jnp.swapaxes(t, 0, 1) for t in (r, w, k, v, a, b))
    _, y = lax.scan(step, jnp.zeros((Bsz, H, N, N), jnp.float32), xs)
    return jnp.swapaxes(y, 0, 1)


def gla_chunked(q, k, v, log_a):
    Bsz, H, T, dk = q.shape
    dv = v.shape[-1]
    C = GLA_CHUNK
    n = T // C
    q, k, log_a = (t.reshape(Bsz, H, n, C, dk) for t in (q, k, log_a))
    v = v.reshape(Bsz, H, n, C, dv)
    b = jnp.cumsum(log_a, axis=3)
    b_last = b[:, :, :, -1:, :]
    q_dec = q * jnp.exp(b)
    scores = jnp.einsum('bhncd,bhnsd->bhncs', q_dec, k * jnp.exp(-b))
    causal = jnp.tril(jnp.ones((C, C), dtype=bool))
    o_intra = jnp.einsum('bhncs,bhnse->bhnce', jnp.where(causal, scores, 0.0), v)
    chunk_kv = jnp.einsum('bhnsd,bhnse->bhnde', k * jnp.exp(b_last - b), v)
    chunk_decay = jnp.exp(b_last[:, :, :, 0, :])

    def step(S, inp):
        kv, dec = inp
        return S * dec[..., None] + kv, S

    _, S_prev = lax.scan(step, jnp.zeros((Bsz, H, dk, dv), jnp.float32),
                         (jnp.moveaxis(chunk_kv, 2, 0), jnp.moveaxis(chunk_decay, 2, 0)))
    o_inter = jnp.einsum('bhncd,nbhde->bhnce', q_dec, S_prev)
    return (o_intra + o_inter).reshape(Bsz, H, T, dv)


def rwkv_gla_mixer(h, w_in, w_out, mu_rkv, mu_wag, w0, w1, w2, a0, a1, a2, g1, g2,
                   k_k, k_a, r_k, ln_w, ln_b, wa1, wa2, ba, gla_norm):
    f32 = jnp.float32
    Bsz, T, _ = h.shape
    proj = h @ w_in
    p_rkv = proj[..., :3 * RW_WIDTH]
    gq, gk, gv, gg = jnp.split(proj[..., 3 * RW_WIDTH:],
                               [GLA_KW, 2 * GLA_KW, 2 * GLA_KW + GLA_VW], axis=-1)

    rkv = p_rkv + (token_shift(p_rkv) - p_rkv) * mu_rkv.reshape(-1)
    r, k, v = [t.astype(f32) for t in jnp.split(rkv, 3, axis=-1)]
    dh = token_shift(h) - h
    xw = h + dh * mu_wag[0]
    xa = h + dh * mu_wag[1]
    xg = h + dh * mu_wag[2]
    w_raw = -jax.nn.softplus(-(w0 + jnp.tanh(xw @ w1) @ w2).astype(f32)) - 0.5
    decay = jnp.exp(-jnp.exp(w_raw))
    a = jax.nn.sigmoid((a0 + (xa @ a1) @ a2).astype(f32))
    g = jax.nn.sigmoid(xg @ g1) @ g2

    def heads(t):
        return t.reshape(Bsz, T, RW_HEADS, RW_HEAD_DIM)

    kk = heads(k * k_k.astype(f32))
    kk = kk / jnp.maximum(jnp.sqrt(jnp.sum(kk * kk, axis=-1, keepdims=True)), 1e-12)
    k = k * (1.0 + (a - 1.0) * k_a.astype(f32))
    rh, kh, vh = heads(r), heads(k), heads(v)
    y = rwkv7_recurrence(rh, heads(decay), kh, vh, -kk, kk * heads(a))
    mean = jnp.mean(y, axis=-1, keepdims=True)
    var = jnp.mean(jnp.square(y - mean), axis=-1, keepdims=True)
    y = ((y - mean) * lax.rsqrt(var + RW_LN_EPS)).reshape(Bsz, T, RW_WIDTH)
    y = y * ln_w.astype(f32) + ln_b.astype(f32)
    bonus = jnp.sum(rh * kh * r_k.astype(f32), axis=-1, keepdims=True) * vh
    o_a = (y + bonus.reshape(Bsz, T, RW_WIDTH)).astype(h.dtype) * g

    def gheads(t, d):
        return t.reshape(Bsz, T, GLA_HEADS, d).transpose(0, 2, 1, 3).astype(f32)

    log_a = jax.nn.log_sigmoid(((h @ wa1) @ wa2 + ba).astype(f32)) / GLA_GATE_NORMALIZER
    o = gla_chunked(gheads(gq, GLA_DK) * (GLA_DK ** -0.5), gheads(gk, GLA_DK),
                    gheads(gv, GLA_DV), gheads(log_a, GLA_DK))
    o = o.transpose(0, 2, 1, 3)
    o = o * lax.rsqrt(jnp.mean(o * o, axis=-1, keepdims=True) + GLA_NORM_EPS)
    o = o * gla_norm.astype(f32).reshape(GLA_HEADS, GLA_DV)
    o_b = o.reshape(Bsz, T, GLA_VW).astype(h.dtype) * jax.nn.silu(gg)

    return jnp.concatenate([o_a, o_b], axis=-1) @ w_out


def diff_attention_core(q1, q2, k1, k2, v, lam, bias_by_dist):
    Bsz, H, T, d = q1.shape
    nb = T // DA_Q_BLOCK
    scale = d ** -0.5

    def to_blocks(t):
        return t.reshape(Bsz, H, nb, DA_Q_BLOCK, d).transpose(2, 0, 1, 3, 4)

    starts = jnp.arange(nb, dtype=jnp.int32) * DA_Q_BLOCK
    k_pos = jnp.arange(T, dtype=jnp.int32)

    def block(args):
        q1b, q2b, start = args
        dist = start + jnp.arange(DA_Q_BLOCK, dtype=jnp.int32)[:, None] - k_pos[None, :]
        causal = dist >= 0
        bias = jnp.transpose(bias_by_dist[jnp.clip(dist, 0, T - 1)], (2, 0, 1)).astype(jnp.float32)

        def probs(qb, kb):
            s = jnp.einsum('bhqd,bhkd->bhqk', qb, kb, preferred_element_type=jnp.float32) * scale + bias
            return jax.nn.softmax(jnp.where(causal, s, NEG_INF), axis=-1)

        attn = probs(q1b, k1) - lam * probs(q2b, k2)
        return jnp.einsum('bhqk,bhke->bhqe', attn.astype(v.dtype), v)

    out = lax.map(block, (to_blocks(q1), to_blocks(q2), starts))
    return out.transpose(1, 2, 0, 3, 4).reshape(Bsz, H, T, v.shape[-1])


def diff_attn_mixer(h, w_qkv, w_out, lam_q1, lam_k1, lam_q2, lam_k2, subln, bias_by_dist, lam_init):
    f32 = jnp.float32
    Bsz, T, _ = h.shape
    q, k, v = jnp.split(h @ w_qkv, [DA_QK_WIDTH, 2 * DA_QK_WIDTH], axis=-1)
    q = q.reshape(Bsz, T, DA_HEADS, 2, DA_QK_DIM).transpose(3, 0, 2, 1, 4)
    k = k.reshape(Bsz, T, DA_HEADS, 2, DA_QK_DIM).transpose(3, 0, 2, 1, 4)
    v = v.reshape(Bsz, T, DA_HEADS, DA_V_DIM).transpose(0, 2, 1, 3)
    lam = (jnp.exp(jnp.sum(lam_q1.astype(f32) * lam_k1.astype(f32)))
           - jnp.exp(jnp.sum(lam_q2.astype(f32) * lam_k2.astype(f32))) + lam_init)
    o = diff_attention_core(q[0], q[1], k[0], k[1], v, lam, bias_by_dist).astype(f32)
    o = o * lax.rsqrt(jnp.mean(o * o, axis=-1, keepdims=True) + DA_SUBLN_EPS)
    o = o * subln.astype(f32) * (1.0 - lam_init)
    o = o.astype(h.dtype).transpose(0, 2, 1, 3).reshape(Bsz, T, DA_V_WIDTH)
    return o @ w_out


def setup_inputs(seed: int = 0) -> dict:
    key = jax.random.key(seed)
    ks = jax.random.split(key, 48)
    counter = [0]
    f32 = jnp.float32

    def nxt():
        kk = ks[counter[0]]
        counter[0] += 1
        return kk

    def nrm(shape, scale):
        return jax.random.normal(nxt(), shape, f32) * scale

    def unif(shape, lo, hi):
        return jax.random.uniform(nxt(), shape, f32, lo, hi)

    def gain(shape):
        return 1.0 + nrm(shape, 0.02)

    D, E, O = D_MODEL, N_EVEN, N_ODD
    return {
        "x": nrm((BATCH, SEQ, D), 1.0),
        "rel_bias": nrm((REL_BUCKETS, DA_HEADS), 0.5),
        "norm_mix": gain((DEPTH, D)),
        "norm_ffn": gain((DEPTH, D)),
        "norm_final": gain((D,)),
        "ab_w_in": nrm((E, D, AB_IN), D ** -0.5),
        "ab_w_out": nrm((E, AB_OUT, D), AB_OUT ** -0.5),
        "rw_mu_rkv": unif((E, 3, RW_WIDTH), 0.0, 1.0),
        "rw_mu_wag": unif((E, 3, D), 0.0, 1.0),
        "rw_w0": unif((E, RW_WIDTH), -6.0, -1.0),
        "rw_w1": nrm((E, D, RW_DECAY_LORA), D ** -0.5),
        "rw_w2": nrm((E, RW_DECAY_LORA, RW_WIDTH), 0.5 * RW_DECAY_LORA ** -0.5),
        "rw_a0": nrm((E, RW_WIDTH), 0.1),
        "rw_a1": nrm((E, D, RW_A_LORA), D ** -0.5),
        "rw_a2": nrm((E, RW_A_LORA, RW_WIDTH), 0.5 * RW_A_LORA ** -0.5),
        "rw_g1": nrm((E, D, RW_GATE_LORA), D ** -0.5),
        "rw_g2": nrm((E, RW_GATE_LORA, RW_WIDTH), RW_GATE_LORA ** -0.5),
        "rw_k_k": 0.85 + nrm((E, RW_WIDTH), 0.02),
        "rw_k_a": 1.0 + nrm((E, RW_WIDTH), 0.02),
        "rw_r_k": -0.04 + nrm((E, RW_HEADS, RW_HEAD_DIM), 0.1),
        "rw_ln_w": gain((E, RW_WIDTH)),
        "rw_ln_b": nrm((E, RW_WIDTH), 0.02),
        "gla_wa1": nrm((E, D, GLA_GATE_LORA), D ** -0.5),
        "gla_wa2": nrm((E, GLA_GATE_LORA, GLA_KW), GLA_GATE_LORA ** -0.5),
        "gla_ba": nrm((E, GLA_KW), 0.1),
        "gla_norm": gain((E, GLA_VW)),
        "da_w_qkv": nrm((O, D, 2 * DA_QK_WIDTH + DA_V_WIDTH), D ** -0.5),
        "da_w_out": nrm((O, DA_V_WIDTH, D), DA_V_WIDTH ** -0.5),
        "da_lam_q1": nrm((O, DA_QK_DIM), 0.1),
        "da_lam_k1": nrm((O, DA_QK_DIM), 0.1),
        "da_lam_q2": nrm((O, DA_QK_DIM), 0.1),
        "da_lam_k2": nrm((O, DA_QK_DIM), 0.1),
        "da_subln": gain((O, DA_V_DIM)),
        "ffn_w_gate": nrm((DEPTH, D, FFN_HIDDEN), D ** -0.5),
        "ffn_w_up": nrm((DEPTH, D, FFN_HIDDEN), D ** -0.5),
        "ffn_w_down": nrm((DEPTH, FFN_HIDDEN, D), FFN_HIDDEN ** -0.5),
    }


def reference(x, rel_bias, norm_mix, norm_ffn, norm_final, ab_w_in, ab_w_out,
              rw_mu_rkv, rw_mu_wag, rw_w0, rw_w1, rw_w2, rw_a0, rw_a1, rw_a2, rw_g1, rw_g2,
              rw_k_k, rw_k_a, rw_r_k, rw_ln_w, rw_ln_b, gla_wa1, gla_wa2, gla_ba, gla_norm,
              da_w_qkv, da_w_out, da_lam_q1, da_lam_k1, da_lam_q2, da_lam_k2, da_subln,
              ffn_w_gate, ffn_w_up, ffn_w_down):
    T = x.shape[1]
    bias_by_dist = rel_bias[t5_causal_buckets(jnp.arange(T, dtype=jnp.int32))]
    for layer in range(DEPTH):
        i = layer // 2
        h = rms_norm(x, norm_mix[layer])
        if layer % 2 == 0:
            mix = rwkv_gla_mixer(h, ab_w_in[i], ab_w_out[i], rw_mu_rkv[i], rw_mu_wag[i],
                                 rw_w0[i], rw_w1[i], rw_w2[i], rw_a0[i], rw_a1[i], rw_a2[i],
                                 rw_g1[i], rw_g2[i], rw_k_k[i], rw_k_a[i], rw_r_k[i],
                                 rw_ln_w[i], rw_ln_b[i], gla_wa1[i], gla_wa2[i], gla_ba[i], gla_norm[i])
        else:
            mix = diff_attn_mixer(h, da_w_qkv[i], da_w_out[i], da_lam_q1[i], da_lam_k1[i],
                                  da_lam_q2[i], da_lam_k2[i], da_subln[i], bias_by_dist,
                                  diff_lambda_init(layer))
        x = x + mix
        x = x + swiglu(rms_norm(x, norm_ffn[layer]), ffn_w_gate[layer], ffn_w_up[layer], ffn_w_down[layer])
    return rms_norm(x, norm_final)
```

```python
import functools
import math

import jax
import jax.numpy as jnp
from jax import lax
from jax.experimental import pallas as pl
from jax.experimental.pallas import tpu as pltpu

F32 = jnp.float32
BF16 = jnp.bfloat16

RW_HEAD_DIM = 64
RW_WIDTH = 512
RW_LN_EPS = 64e-5
GLA_HEADS = 4
GLA_DK = 64
GLA_DV = 128
GLA_KW = 256
GLA_VW = 512
GLA_GATE_NORMALIZER = 16.0
GLA_NORM_EPS = 1e-5
DA_QK_DIM = 64
DA_V_DIM = 128
DA_SUBLN_EPS = 1e-5
NEG_INF = -1e30
REL_BUCKETS = 32
REL_MAX_DIST = 128
NORM_EPS = 1e-6

LANES = 128
CHUNK = 64
VMEM_LIMIT = 56 * 1024 * 1024

TM_PROJ = 256
TM_FFN = 512
FFN_HC = 256
TB_REC = 256
T_ATT = 256


def _dot(a, b):
    return jnp.dot(a.astype(BF16), b.astype(BF16), preferred_element_type=F32)


def _dot_nt(a, b):
    return lax.dot_general(a.astype(BF16), b.astype(BF16),
                           (((1,), (1,)), ((), ())), preferred_element_type=F32)


def _split3(x):
    hi = x.astype(BF16)
    r1 = x - hi.astype(F32)
    mid = r1.astype(BF16)
    lo = (r1 - mid.astype(F32)).astype(BF16)
    return hi, mid, lo


def _exact_dot(m_bf16, x):
    w = x.shape[-1]
    parts = jnp.concatenate(_split3(x), axis=-1)
    y = jnp.dot(m_bf16, parts, preferred_element_type=F32)
    return y[:, :w] + y[:, w:2 * w] + y[:, 2 * w:]


def _exact_dot_r(x, m_bf16):
    hi, mid, lo = _split3(x)
    parts = jnp.concatenate([hi, mid, lo], axis=0)
    y = jnp.dot(parts, m_bf16, preferred_element_type=F32)
    n = x.shape[0]
    return y[:n] + y[n:2 * n] + y[2 * n:]


def _rms_norm(x, g, eps):
    y = x * lax.rsqrt(jnp.mean(x * x, axis=-1, keepdims=True) + eps)
    return y * g


def _softplus(z):
    return jnp.maximum(z, 0.0) + jnp.log(1.0 + jnp.exp(-jnp.abs(z)))


def _sigmoid(z):
    return 1.0 / (1.0 + jnp.exp(-z))


def _const_spec(shape):
    nd = len(shape)
    return pl.BlockSpec(shape, lambda *_: (0,) * nd, pipeline_mode=pl.Buffered(1))


def _params(*sem):
    return pltpu.CompilerParams(dimension_semantics=sem, vmem_limit_bytes=VMEM_LIMIT)


def _shift_rows(cur, prev_row):
    rolled = pltpu.roll(cur, shift=1, axis=0)
    row = lax.broadcasted_iota(jnp.int32, cur.shape, 0)
    return jnp.where(row == 0, prev_row, rolled)


def _even_in_kernel(x_ref, gn_ref, win_ref, mu_wag_ref, mu_rkv_ref,
                    w1_ref, w2_ref, w0_ref, a1_ref, a2_ref, a0_ref,
                    g1_ref, g2_ref, wa2_ref, ba_ref,
                    r_ref, k_ref, v_ref, lw_ref, eta_ref, g_ref,
                    gq_ref, gk_ref, gv_ref, gg_ref, la_ref,
                    carry_h, carry_p, *, tiles_per_seq):
    tm = x_ref.shape[0]

    @pl.when(pl.program_id(0) % tiles_per_seq == 0)
    def _():
        carry_h[...] = jnp.zeros_like(carry_h)
        carry_p[...] = jnp.zeros_like(carry_p)

    h = _rms_norm(x_ref[...], gn_ref[...], NORM_EPS)
    h_prev = _shift_rows(h, carry_h[0:1, :])
    carry_h[0:1, :] = h[tm - 1:tm, :]
    dh = h_prev - h
    xw = h + dh * mu_wag_ref[0:1, :]
    xa = h + dh * mu_wag_ref[1:2, :]
    xg = h + dh * mu_wag_ref[2:3, :]

    proj = _dot(h, win_ref[...])
    p_rkv = proj[:, :3 * RW_WIDTH]
    p_prev = _shift_rows(p_rkv, carry_p[0:1, :])
    carry_p[0:1, :] = p_rkv[tm - 1:tm, :]
    rkv = p_rkv + (p_prev - p_rkv) * mu_rkv_ref[...]
    r_ref[...] = rkv[:, :RW_WIDTH]
    k_ref[...] = rkv[:, RW_WIDTH:2 * RW_WIDTH]
    v_ref[...] = rkv[:, 2 * RW_WIDTH:]

    w_lin = w0_ref[...] + _dot(jnp.tanh(_dot(xw, w1_ref[...])), w2_ref[...])
    w_raw = -_softplus(-w_lin) - 0.5
    lw_ref[...] = -jnp.exp(w_raw)
    eta_ref[...] = _sigmoid(a0_ref[...] + _dot(_dot(xa, a1_ref[...]), a2_ref[...]))
    g_ref[...] = _dot(_sigmoid(_dot(xg, g1_ref[...])), g2_ref[...])

    o = 3 * RW_WIDTH
    gq_ref[...] = proj[:, o:o + GLA_KW]
    gk_ref[...] = proj[:, o + GLA_KW:o + 2 * GLA_KW]
    gv_ref[...] = proj[:, o + 2 * GLA_KW:o + 2 * GLA_KW + GLA_VW]
    gg_ref[...] = proj[:, o + 2 * GLA_KW + GLA_VW:o + 2 * GLA_KW + 2 * GLA_VW]
    lora = proj[:, o + 2 * GLA_KW + 2 * GLA_VW:]
    la_lin = _dot(lora, wa2_ref[...]) + ba_ref[...]
    la_ref[...] = -_softplus(-la_lin) / GLA_GATE_NORMALIZER


def _pad_cols(w, n):
    return jnp.pad(w, ((0, 0), (0, n - w.shape[1])))


def _pad_rows(w, n):
    return jnp.pad(w, ((0, n - w.shape[0]), (0, 0)))


def _even_in(x2, seq_len, gn, w_in, mu_rkv, mu_wag, w0, w1, w2, a0, a1, a2, g1, g2,
             wa1, wa2, ba):
    n, d = x2.shape
    tm = TM_PROJ
    win = jnp.concatenate([w_in, _pad_cols(wa1, LANES)], axis=1).astype(BF16)
    w1p, w2p = _pad_cols(w1, LANES).astype(BF16), _pad_rows(w2, LANES).astype(BF16)
    a1p, a2p = _pad_cols(a1, LANES).astype(BF16), _pad_rows(a2, LANES).astype(BF16)
    g1p, g2p = _pad_cols(g1, 2 * LANES).astype(BF16), _pad_rows(g2, 2 * LANES).astype(BF16)
    wa2p = _pad_rows(wa2, LANES).astype(BF16)
    consts = [gn.reshape(1, d), win, mu_wag, mu_rkv.reshape(1, -1),
              w1p, w2p, w0.reshape(1, -1), a1p, a2p, a0.reshape(1, -1),
              g1p, g2p, wa2p, ba.reshape(1, -1)]
    widths = [RW_WIDTH] * 6 + [GLA_KW, GLA_KW, GLA_VW, GLA_VW, GLA_KW]
    row = lambda w: pl.BlockSpec((tm, w), lambda i: (i, 0))
    return pl.pallas_call(
        functools.partial(_even_in_kernel, tiles_per_seq=seq_len // tm),
        grid=(n // tm,),
        in_specs=[row(d)] + [_const_spec(c.shape) for c in consts],
        out_specs=[row(w) for w in widths],
        out_shape=[jax.ShapeDtypeStruct((n, w), F32) for w in widths],
        scratch_shapes=[pltpu.VMEM((8, d), F32), pltpu.VMEM((8, 3 * RW_WIDTH), F32)],
        compiler_params=_params("arbitrary"),
        name="even_in",
    )(x2, *consts)


def _lane_half_masks(shape):
    lane = lax.broadcasted_iota(jnp.int32, shape, len(shape) - 1)
    return lane < 64, lane >= 64


def _stack_heads(x):
    m0, m1 = _lane_half_masks(x.shape)
    return jnp.concatenate([jnp.where(m0, x, 0.0), jnp.where(m1, x, 0.0)], axis=0)


def _tri_masks(n, blk):
    row = lax.broadcasted_iota(jnp.int32, (n, n), 0)
    col = lax.broadcasted_iota(jnp.int32, (n, n), 1)
    shift = int(math.log2(blk))
    same = jnp.right_shift(row, shift) == jnp.right_shift(col, shift)
    return same & (col < row), same & (col <= row)


def _rwkv_kernel(r_ref, k_ref, v_ref, lw_ref, eta_ref, g_ref,
                 kk_ref, ka_ref, rk_ref, lnw_ref, lnb_ref, o_ref, h_ref):
    t = pl.program_id(1)

    @pl.when(t == 0)
    def _():
        h_ref[...] = jnp.zeros_like(h_ref)

    c = CHUNK
    n_pairs = RW_WIDTH // LANES
    strict, incl = _tri_masks(2 * c, c)
    row = lax.broadcasted_iota(jnp.int32, (c, c), 0)
    col = lax.broadcasted_iota(jnp.int32, (c, c), 1)
    cum_mat = (col <= row).astype(BF16)
    r2 = lax.broadcasted_iota(jnp.int32, (LANES, LANES), 0)
    c2 = lax.broadcasted_iota(jnp.int32, (LANES, LANES), 1)
    head_ones = (jnp.right_shift(r2, 6) == jnp.right_shift(c2, 6)).astype(BF16)
    eye = (r2 == c2).astype(F32)

    def chunk_body(ci, carry):
        rows = pl.ds(pl.multiple_of(ci * c, c), c)
        for p in range(n_pairs):
            lanes = slice(p * LANES, (p + 1) * LANES)
            r = r_ref[rows, lanes]
            k = k_ref[rows, lanes]
            v = v_ref[rows, lanes]
            lw = lw_ref[rows, lanes]
            eta = eta_ref[rows, lanes]

            kk = k * kk_ref[:, lanes]
            kk = kk / jnp.maximum(jnp.sqrt(_exact_dot_r(kk * kk, head_ones)), 1e-12)
            k2 = k * (1.0 + (eta - 1.0) * ka_ref[:, lanes])
            a_vec = -kk
            b_vec = kk * eta

            cum = _exact_dot(cum_mat, lw)
            cum_end = cum[c - 1:c, :]
            e_neg = jnp.exp(-cum)
            e_rel = jnp.exp(cum_end - cum)
            a_st = _stack_heads(a_vec * jnp.exp(cum - lw))
            r_st = _stack_heads(r * jnp.exp(cum))
            k_st = _stack_heads(k2 * e_neg)
            b_st = _stack_heads(b_vec * e_neg)
            kh_st = _stack_heads(k2 * e_rel)
            bh_st = _stack_heads(b_vec * e_rel)
            v_st = _stack_heads(v)

            big = _dot_nt(jnp.concatenate([a_st, r_st], axis=0),
                          jnp.concatenate([b_st, k_st], axis=0))
            a_ab = jnp.where(strict, big[:2 * c, :2 * c], 0.0)
            a_ak = jnp.where(strict, big[:2 * c, 2 * c:], 0.0)
            a_rb = jnp.where(incl, big[2 * c:, :2 * c], 0.0)
            a_rk = jnp.where(incl, big[2 * c:, 2 * c:], 0.0)

            m = a_ab
            inv = eye + a_ab
            for _ in range(int(math.log2(c)) - 1):
                m = _dot(m, m)
                inv = inv + _dot(inv, m)

            h0 = h_ref[p]
            u = _dot(inv, _dot(a_st, h0) + _dot(a_ak, v_st))
            y_st = _dot(r_st, h0) + _dot(a_rk, v_st) + _dot(a_rb, u)
            gam = jnp.exp(jnp.broadcast_to(cum_end, (LANES, LANES))).T
            h_ref[p] = gam * h0 + _dot(kh_st.T, v_st) + _dot(bh_st.T, u)
            y = y_st[:c] + y_st[c:]

            mean = _exact_dot_r(y, head_ones) / RW_HEAD_DIM
            yc = y - mean
            var = _exact_dot_r(yc * yc, head_ones) / RW_HEAD_DIM
            yn = yc * lax.rsqrt(var + RW_LN_EPS) * lnw_ref[:, lanes] + lnb_ref[:, lanes]
            bonus = _exact_dot_r(r * k2 * rk_ref[:, lanes], head_ones) * v
            o_ref[rows, lanes] = ((yn + bonus) * g_ref[rows, lanes]).astype(o_ref.dtype)
        return carry

    lax.fori_loop(0, r_ref.shape[0] // c, chunk_body, 0)


def _rwkv(batch, seq_len, r, k, v, lw, eta, g, k_k, k_a, r_k, ln_w, ln_b):
    n = r.shape[0]
    tb = TB_REC
    nt = seq_len // tb
    row = pl.BlockSpec((tb, RW_WIDTH), lambda b, t: (b * nt + t, 0))
    vecs = [p.reshape(1, RW_WIDTH) for p in (k_k, k_a, r_k, ln_w, ln_b)]
    return pl.pallas_call(
        _rwkv_kernel,
        grid=(batch, nt),
        in_specs=[row] * 6 + [_const_spec((1, RW_WIDTH))] * 5,
        out_specs=row,
        out_shape=jax.ShapeDtypeStruct((n, RW_WIDTH), BF16),
        scratch_shapes=[pltpu.VMEM((RW_WIDTH // LANES, LANES, LANES), F32)],
        compiler_params=_params("arbitrary", "arbitrary"),
        name="rwkv7",
    )(r, k, v, lw, eta, g, *vecs)


def _gla_kernel(q_ref, k_ref, v_ref, gg_ref, la_ref, nrm_ref, o_ref, s_ref):
    t = pl.program_id(1)

    @pl.when(t == 0)
    def _():
        s_ref[...] = jnp.zeros_like(s_ref)

    c = CHUNK
    n_pairs = GLA_KW // LANES
    row = lax.broadcasted_iota(jnp.int32, (c, c), 0)
    col = lax.broadcasted_iota(jnp.int32, (c, c), 1)
    cum_mat = (col <= row).astype(BF16)
    row2 = lax.broadcasted_iota(jnp.int32, (2 * c, c), 0)
    col2 = lax.broadcasted_iota(jnp.int32, (2 * c, c), 1)
    causal = col2 <= jnp.bitwise_and(row2, c - 1)
    scale = GLA_DK ** -0.5

    def chunk_body(ci, carry):
        rows = pl.ds(pl.multiple_of(ci * c, c), c)
        for p in range(n_pairs):
            lanes = slice(p * LANES, (p + 1) * LANES)
            q = q_ref[rows, lanes] * scale
            k = k_ref[rows, lanes]
            la = la_ref[rows, lanes]
            cum = _exact_dot(cum_mat, la)
            cum_end = cum[c - 1:c, :]
            q_st = _stack_heads(q * jnp.exp(cum))
            k_dec = k * jnp.exp(-cum)
            kh_st = _stack_heads(k * jnp.exp(cum_end - cum))
            scores = jnp.where(causal, _dot_nt(q_st, k_dec), 0.0)
            s0 = s_ref[p]
            inter = _dot(q_st, s0)
            gam = jnp.exp(jnp.broadcast_to(cum_end, (LANES, LANES))).T
            v_pair = v_ref[rows, 2 * p * GLA_DV:(2 * p + 2) * GLA_DV]
            v_st = jnp.concatenate([v_pair[:, :GLA_DV], v_pair[:, GLA_DV:]], axis=0)
            s_ref[p] = gam * s0 + _dot(kh_st.T, v_st)
            for hh in range(2):
                head = 2 * p + hh
                vl = slice(head * GLA_DV, (head + 1) * GLA_DV)
                hr = slice(hh * c, (hh + 1) * c)
                o = _dot(scores[hr], v_st[hr]) + inter[hr]
                o = o * lax.rsqrt(jnp.mean(o * o, axis=-1, keepdims=True) + GLA_NORM_EPS)
                o = o * nrm_ref[:, vl]
                gg = gg_ref[rows, vl]
                o_ref[rows, vl] = (o * (gg * _sigmoid(gg))).astype(o_ref.dtype)
        return carry

    lax.fori_loop(0, q_ref.shape[0] // c, chunk_body, 0)


def _gla(batch, seq_len, gq, gk, gv, gg, la, gla_norm):
    n = gq.shape[0]
    tb = TB_REC
    nt = seq_len // tb
    rowk = pl.BlockSpec((tb, GLA_KW), lambda b, t: (b * nt + t, 0))
    rowv = pl.BlockSpec((tb, GLA_VW), lambda b, t: (b * nt + t, 0))
    return pl.pallas_call(
        _gla_kernel,
        grid=(batch, nt),
        in_specs=[rowk, rowk, rowv, rowv, rowk, _const_spec((1, GLA_VW))],
        out_specs=rowv,
        out_shape=jax.ShapeDtypeStruct((n, GLA_VW), BF16),
        scratch_shapes=[pltpu.VMEM((GLA_KW // LANES, LANES, GLA_DV), F32)],
        compiler_params=_params("arbitrary", "arbitrary"),
        name="gla",
    )(gq, gk, gv, gg, la, gla_norm.reshape(1, GLA_VW))


def _odd_in_kernel(x_ref, gn_ref, w_ref, q_ref, k_ref, v_ref):
    h = _rms_norm(x_ref[...], gn_ref[...], NORM_EPS)
    qkv = _dot(h, w_ref[...])
    d = q_ref.shape[1]
    q_ref[...] = (qkv[:, :d] * (DA_QK_DIM ** -0.5)).astype(q_ref.dtype)
    k_ref[...] = qkv[:, d:2 * d].astype(k_ref.dtype)
    v_ref[...] = qkv[:, 2 * d:].astype(v_ref.dtype)


def _odd_in(x2, gn, w_qkv):
    n, d = x2.shape
    tm = TM_PROJ
    row = pl.BlockSpec((tm, d), lambda i: (i, 0))
    return pl.pallas_call(
        _odd_in_kernel,
        grid=(n // tm,),
        in_specs=[row, _const_spec((1, d)), _const_spec(w_qkv.shape)],
        out_specs=[row] * 3,
        out_shape=[jax.ShapeDtypeStruct((n, d), BF16)] * 3,
        compiler_params=_params("arbitrary"),
        name="odd_in",
    )(x2, gn.reshape(1, d), w_qkv.astype(BF16))


def _attn_kernel(lam_ref, q_ref, k_ref, v_ref, bias_ref, cfar_ref, subln_ref, o_ref,
                 m_ref, l_ref, acc_ref, *, lam_init):
    i = pl.program_id(2)
    tq = q_ref.shape[0]
    tk = tq

    q_st = _stack_heads(q_ref[...].astype(F32)).astype(BF16)
    m_ref[...] = jnp.full_like(m_ref, -jnp.inf)
    l_ref[...] = jnp.zeros_like(l_ref)
    acc_ref[...] = jnp.zeros_like(acc_ref)
    c_far = cfar_ref[0, 0:1, 0:1]

    def step(j, bias):
        rows = pl.ds(pl.multiple_of(j * tk, tk), tk)
        s = _dot_nt(q_st, k_ref[rows, :])
        m_prev = m_ref[...]
        if bias is None:
            m_new = jnp.maximum(m_prev, jnp.max(s, axis=1, keepdims=True) + c_far)
            p = jnp.exp(s - (m_new - c_far))
        else:
            s = s + bias
            m_new = jnp.maximum(m_prev, jnp.max(s, axis=1, keepdims=True))
            p = jnp.exp(s - m_new)
        alpha = jnp.exp(m_prev - m_new)
        l_ref[...] = alpha * l_ref[...] + jnp.sum(p, axis=1, keepdims=True)
        acc_ref[...] = alpha * acc_ref[...] + _dot(p, v_ref[rows, :])
        m_ref[...] = m_new

    def far_body(j, carry):
        step(j, None)
        return carry

    lax.fori_loop(0, jnp.maximum(i - 1, 0), far_body, 0)

    @pl.when(i >= 1)
    def _():
        step(i - 1, bias_ref[0, 1])

    step(i, bias_ref[0, 0])

    lam_v = lam_ref[...]
    lam = (jnp.exp(jnp.sum(lam_v[0:1] * lam_v[1:2], axis=1, keepdims=True))
           - jnp.exp(jnp.sum(lam_v[2:3] * lam_v[3:4], axis=1, keepdims=True)) + lam_init)
    o_all = acc_ref[...] / l_ref[...]
    o = o_all[:tq] - lam * o_all[tq:]
    o = o * lax.rsqrt(jnp.mean(o * o, axis=-1, keepdims=True) + DA_SUBLN_EPS)
    o = o * subln_ref[...] * (1.0 - lam_init)
    o_ref[...] = o.astype(o_ref.dtype)


def _t5_causal_buckets(dist):
    max_exact = REL_BUCKETS // 2
    ratio = jnp.maximum(dist, max_exact).astype(F32) / max_exact
    large = max_exact + (jnp.log(ratio) / math.log(REL_MAX_DIST / max_exact)
                         * (REL_BUCKETS - max_exact)).astype(jnp.int32)
    large = jnp.minimum(large, REL_BUCKETS - 1)
    return jnp.where(dist < max_exact, dist, large)


def _attn_bias_tiles(rel_bias, seq_len, tq):
    bias_by_dist = rel_bias[_t5_causal_buckets(jnp.arange(seq_len, dtype=jnp.int32))]
    qi = jnp.arange(tq, dtype=jnp.int32)[:, None]
    kj = jnp.arange(tq, dtype=jnp.int32)[None, :]
    diag = jnp.where((kj <= qi)[..., None], bias_by_dist[jnp.clip(qi - kj, 0, seq_len - 1)], NEG_INF)
    prev = bias_by_dist[tq + qi - kj]
    tiles = jnp.stack([diag, prev], axis=0).transpose(3, 0, 1, 2)
    tiles = jnp.concatenate([tiles, tiles], axis=2).astype(F32)
    c_far = jnp.broadcast_to(rel_bias[REL_BUCKETS - 1][:, None, None], (rel_bias.shape[1], 8, LANES))
    return tiles, c_far.astype(F32)


def _diff_attn(batch, seq_len, q, k, v, lam_params, subln, bias_tiles, c_far, lam_init):
    n, d = q.shape
    heads = d // DA_V_DIM
    tq = T_ATT
    nq = seq_len // tq
    assert tq > REL_MAX_DIST
    qspec = pl.BlockSpec((tq, DA_V_DIM), lambda b, h, i: (b * nq + i, h))
    kvspec = pl.BlockSpec((seq_len, DA_V_DIM), lambda b, h, i: (b, h))
    return pl.pallas_call(
        functools.partial(_attn_kernel, lam_init=lam_init),
        grid=(batch, heads, nq),
        in_specs=[_const_spec(lam_params.shape), qspec, kvspec, kvspec,
                  pl.BlockSpec((1, 2, 2 * tq, tq), lambda b, h, i: (h, 0, 0, 0)),
                  pl.BlockSpec((1, 8, LANES), lambda b, h, i: (h, 0, 0)),
                  _const_spec((1, DA_V_DIM))],
        out_specs=qspec,
        out_shape=jax.ShapeDtypeStruct((n, d), BF16),
        scratch_shapes=[pltpu.VMEM((2 * tq, 1), F32), pltpu.VMEM((2 * tq, 1), F32),
                        pltpu.VMEM((2 * tq, DA_V_DIM), F32)],
        compiler_params=_params("arbitrary", "arbitrary", "arbitrary"),
        name="diff_attn",
    )(lam_params, q, k, v, bias_tiles, c_far, subln.reshape(1, DA_V_DIM))


def _ffn_kernel(*refs, n_mix, final):
    x_ref = refs[0]
    mix_refs = refs[1:1 + n_mix]
    wout_ref, gn_ref, wg_ref, wu_ref, wd_ref = refs[1 + n_mix:6 + n_mix]
    gfin_ref = refs[6 + n_mix] if final else None
    o_ref = refs[-1]

    mix = jnp.concatenate([m_ref[...] for m_ref in mix_refs], axis=1)
    x = x_ref[...] + jnp.dot(mix, wout_ref[...], preferred_element_type=F32)
    h = _rms_norm(x, gn_ref[...], NORM_EPS).astype(BF16)
    acc = x
    hidden = wg_ref.shape[1]
    for c0 in range(0, hidden, FFN_HC):
        cols = slice(c0, c0 + FFN_HC)
        gate = jnp.dot(h, wg_ref[:, cols], preferred_element_type=F32)
        up = jnp.dot(h, wu_ref[:, cols], preferred_element_type=F32)
        act = (gate * _sigmoid(gate)) * up
        acc = acc + jnp.dot(act.astype(BF16), wd_ref[cols, :], preferred_element_type=F32)
    if final:
        acc = _rms_norm(acc, gfin_ref[...], NORM_EPS)
    o_ref[...] = acc


def _ffn(x2, mixes, w_out, gn, wg, wu, wd, g_final=None):
    n, d = x2.shape
    tm = TM_FFN
    final = g_final is not None
    row = lambda w: pl.BlockSpec((tm, w), lambda i: (i, 0))
    consts = [w_out.astype(BF16), gn.reshape(1, d), wg.astype(BF16), wu.astype(BF16),
              wd.astype(BF16)]
    if final:
        consts.append(g_final.reshape(1, d))
    return pl.pallas_call(
        functools.partial(_ffn_kernel, n_mix=len(mixes), final=final),
        grid=(n // tm,),
        in_specs=([row(d)] + [row(m.shape[1]) for m in mixes]
                  + [_const_spec(c.shape) for c in consts]),
        out_specs=row(d),
        out_shape=jax.ShapeDtypeStruct((n, d), F32),
        compiler_params=_params("arbitrary"),
        name="ffn",
    )(x2, *mixes, *consts)


def _diff_lambda_init(layer):
    return 0.8 - 0.6 * math.exp(-0.3 * layer)


def kernel(x, rel_bias, norm_mix, norm_ffn, norm_final, ab_w_in, ab_w_out, rw_mu_rkv, rw_mu_wag, rw_w0, rw_w1, rw_w2, rw_a0, rw_a1, rw_a2, rw_g1, rw_g2, rw_k_k, rw_k_a, rw_r_k, rw_ln_w, rw_ln_b, gla_wa1, gla_wa2, gla_ba, gla_norm, da_w_qkv, da_w_out, da_lam_q1, da_lam_k1, da_lam_q2, da_lam_k2, da_subln, ffn_w_gate, ffn_w_up, ffn_w_down):
    batch, seq_len, d = x.shape
    depth = norm_mix.shape[0]
    x2 = x.reshape(batch * seq_len, d)
    bias_tiles, c_far = _attn_bias_tiles(rel_bias, seq_len, T_ATT)
    for layer in range(depth):
        i = layer // 2
        if layer % 2 == 0:
            (r, k, v, lw, eta, g, gq, gk, gv, gg, la) = _even_in(
                x2, seq_len, norm_mix[layer], ab_w_in[i], rw_mu_rkv[i], rw_mu_wag[i],
                rw_w0[i], rw_w1[i], rw_w2[i], rw_a0[i], rw_a1[i], rw_a2[i], rw_g1[i], rw_g2[i],
                gla_wa1[i], gla_wa2[i], gla_ba[i])
            o_a = _rwkv(batch, seq_len, r, k, v, lw, eta, g, rw_k_k[i], rw_k_a[i],
                        rw_r_k[i], rw_ln_w[i], rw_ln_b[i])
            o_b = _gla(batch, seq_len, gq, gk, gv, gg, la, gla_norm[i])
            mixes = [o_a, o_b]
            w_out = ab_w_out[i]
        else:
            q, k, v = _odd_in(x2, norm_mix[layer], da_w_qkv[i])
            lam_params = jnp.stack([da_lam_q1[i], da_lam_k1[i], da_lam_q2[i], da_lam_k2[i]])
            o = _diff_attn(batch, seq_len, q, k, v, lam_params, da_subln[i], bias_tiles, c_far,
                           _diff_lambda_init(layer))
            mixes = [o]
            w_out = da_w_out[i]
        x2 = _ffn(x2, mixes, w_out, norm_ffn[layer], ffn_w_gate[layer], ffn_w_up[layer],
                  ffn_w_down[layer], norm_final if layer == depth - 1 else None)
    return x2.reshape(batch, seq_len, d)
```

```python
import functools
import math

import jax
import jax.numpy as jnp
from jax import lax
from jax.experimental import pallas as pl
from jax.experimental.pallas import tpu as pltpu

F32 = jnp.float32
BF16 = jnp.bfloat16

RW_HEAD_DIM = 64
RW_WIDTH = 512
RW_LN_EPS = 64e-5
GLA_HEADS = 4
GLA_DK = 64
GLA_DV = 128
GLA_KW = 256
GLA_VW = 512
GLA_GATE_NORMALIZER = 16.0
GLA_NORM_EPS = 1e-5
DA_QK_DIM = 64
DA_V_DIM = 128
DA_SUBLN_EPS = 1e-5
NEG_INF = -1e30
REL_BUCKETS = 32
REL_MAX_DIST = 128
NORM_EPS = 1e-6

LANES = 128
CHUNK = 64
VMEM_LIMIT = 56 * 1024 * 1024

TM_PROJ = 256
TM_FFN = 512
FFN_HC = 256
TB_REC = 256
T_ATT = 256


def _dot(a, b):
    return jnp.dot(a.astype(BF16), b.astype(BF16), preferred_element_type=F32)


def _dot_nt(a, b):
    return lax.dot_general(a.astype(BF16), b.astype(BF16),
                           (((1,), (1,)), ((), ())), preferred_element_type=F32)


def _split3(x):
    hi = x.astype(BF16)
    r1 = x - hi.astype(F32)
    mid = r1.astype(BF16)
    lo = (r1 - mid.astype(F32)).astype(BF16)
    return hi, mid, lo


def _exact_dot(m_bf16, x):
    w = x.shape[-1]
    parts = jnp.concatenate(_split3(x), axis=-1)
    y = jnp.dot(m_bf16, parts, preferred_element_type=F32)
    return y[:, :w] + y[:, w:2 * w] + y[:, 2 * w:]


def _exact_dot_r(x, m_bf16):
    hi, mid, lo = _split3(x)
    parts = jnp.concatenate([hi, mid, lo], axis=0)
    y = jnp.dot(parts, m_bf16, preferred_element_type=F32)
    n = x.shape[0]
    return y[:n] + y[n:2 * n] + y[2 * n:]


def _rms_norm(x, g, eps):
    y = x * lax.rsqrt(jnp.mean(x * x, axis=-1, keepdims=True) + eps)
    return y * g


def _softplus(z):
    return jnp.maximum(z, 0.0) + jnp.log(1.0 + jnp.exp(-jnp.abs(z)))


def _sigmoid(z):
    return 1.0 / (1.0 + jnp.exp(-z))


def _const_spec(shape):
    nd = len(shape)
    return pl.BlockSpec(shape, lambda *_: (0,) * nd, pipeline_mode=pl.Buffered(1))


def _params(*sem):
    return pltpu.CompilerParams(dimension_semantics=sem, vmem_limit_bytes=VMEM_LIMIT)


def _shift_rows(cur, prev_row):
    rolled = pltpu.roll(cur, shift=1, axis=0)
    row = lax.broadcasted_iota(jnp.int32, cur.shape, 0)
    return jnp.where(row == 0, prev_row, rolled)


def _even_in_kernel(x_ref, gn_ref, win_ref, mu_wag_ref, mu_rkv_ref,
                    w1_ref, w2_ref, w0_ref, a1_ref, a2_ref, a0_ref,
                    g1_ref, g2_ref, wa2_ref, ba_ref,
                    r_ref, k_ref, v_ref, lw_ref, eta_ref, g_ref,
                    gq_ref, gk_ref, gv_ref, gg_ref, la_ref,
                    carry_h, carry_p, *, tiles_per_seq):
    tm = x_ref.shape[0]

    @pl.when(pl.program_id(0) % tiles_per_seq == 0)
    def _():
        carry_h[...] = jnp.zeros_like(carry_h)
        carry_p[...] = jnp.zeros_like(carry_p)

    h = _rms_norm(x_ref[...], gn_ref[...], NORM_EPS)
    h_prev = _shift_rows(h, carry_h[0:1, :])
    carry_h[0:1, :] = h[tm - 1:tm, :]
    dh = h_prev - h
    xw = h + dh * mu_wag_ref[0:1, :]
    xa = h + dh * mu_wag_ref[1:2, :]
    xg = h + dh * mu_wag_ref[2:3, :]

    proj = _dot(h, win_ref[...])
    p_rkv = proj[:, :3 * RW_WIDTH]
    p_prev = _shift_rows(p_rkv, carry_p[0:1, :])
    carry_p[0:1, :] = p_rkv[tm - 1:tm, :]
    rkv = p_rkv + (p_prev - p_rkv) * mu_rkv_ref[...]
    r_ref[...] = rkv[:, :RW_WIDTH]
    k_ref[...] = rkv[:, RW_WIDTH:2 * RW_WIDTH]
    v_ref[...] = rkv[:, 2 * RW_WIDTH:]

    w_lin = w0_ref[...] + _dot(jnp.tanh(_dot(xw, w1_ref[...])), w2_ref[...])
    w_raw = -_softplus(-w_lin) - 0.5
    lw_ref[...] = -jnp.exp(w_raw)
    eta_ref[...] = _sigmoid(a0_ref[...] + _dot(_dot(xa, a1_ref[...]), a2_ref[...]))
    g_ref[...] = _dot(_sigmoid(_dot(xg, g1_ref[...])), g2_ref[...])

    o = 3 * RW_WIDTH
    gq_ref[...] = proj[:, o:o + GLA_KW]
    gk_ref[...] = proj[:, o + GLA_KW:o + 2 * GLA_KW]
    gv_ref[...] = proj[:, o + 2 * GLA_KW:o + 2 * GLA_KW + GLA_VW]
    gg_ref[...] = proj[:, o + 2 * GLA_KW + GLA_VW:o + 2 * GLA_KW + 2 * GLA_VW]
    lora = proj[:, o + 2 * GLA_KW + 2 * GLA_VW:]
    la_lin = _dot(lora, wa2_ref[...]) + ba_ref[...]
    la_ref[...] = -_softplus(-la_lin) / GLA_GATE_NORMALIZER


def _pad_cols(w, n):
    return jnp.pad(w, ((0, 0), (0, n - w.shape[1])))


def _pad_rows(w, n):
    return jnp.pad(w, ((0, n - w.shape[0]), (0, 0)))


def _even_in(x2, seq_len, gn, w_in, mu_rkv, mu_wag, w0, w1, w2, a0, a1, a2, g1, g2,
             wa1, wa2, ba):
    n, d = x2.shape
    tm = TM_PROJ
    win = jnp.concatenate([w_in, _pad_cols(wa1, LANES)], axis=1).astype(BF16)
    w1p, w2p = _pad_cols(w1, LANES).astype(BF16), _pad_rows(w2, LANES).astype(BF16)
    a1p, a2p = _pad_cols(a1, LANES).astype(BF16), _pad_rows(a2, LANES).astype(BF16)
    g1p, g2p = _pad_cols(g1, 2 * LANES).astype(BF16), _pad_rows(g2, 2 * LANES).astype(BF16)
    wa2p = _pad_rows(wa2, LANES).astype(BF16)
    consts = [gn.reshape(1, d), win, mu_wag, mu_rkv.reshape(1, -1),
              w1p, w2p, w0.reshape(1, -1), a1p, a2p, a0.reshape(1, -1),
              g1p, g2p, wa2p, ba.reshape(1, -1)]
    widths = [RW_WIDTH] * 6 + [GLA_KW, GLA_KW, GLA_VW, GLA_VW, GLA_KW]
    row = lambda w: pl.BlockSpec((tm, w), lambda i: (i, 0))
    return pl.pallas_call(
        functools.partial(_even_in_kernel, tiles_per_seq=seq_len // tm),
        grid=(n // tm,),
        in_specs=[row(d)] + [_const_spec(c.shape) for c in consts],
        out_specs=[row(w) for w in widths],
        out_shape=[jax.ShapeDtypeStruct((n, w), F32) for w in widths],
        scratch_shapes=[pltpu.VMEM((8, d), F32), pltpu.VMEM((8, 3 * RW_WIDTH), F32)],
        compiler_params=_params("arbitrary"),
        name="even_in",
    )(x2, *consts)


def _lane_half_masks(shape):
    lane = lax.broadcasted_iota(jnp.int32, shape, len(shape) - 1)
    return lane < 64, lane >= 64


def _stack_heads(x):
    m0, m1 = _lane_half_masks(x.shape)
    return jnp.concatenate([jnp.where(m0, x, 0.0), jnp.where(m1, x, 0.0)], axis=0)


def _tri_masks(n, blk):
    row = lax.broadcasted_iota(jnp.int32, (n, n), 0)
    col = lax.broadcasted_iota(jnp.int32, (n, n), 1)
    shift = int(math.log2(blk))
    same = jnp.right_shift(row, shift) == jnp.right_shift(col, shift)
    return same & (col < row), same & (col <= row)


def _rwkv_kernel(r_ref, k_ref, v_ref, lw_ref, eta_ref, g_ref,
                 kk_ref, ka_ref, rk_ref, lnw_ref, lnb_ref, o_ref, h_ref):
    t = pl.program_id(1)

    @pl.when(t == 0)
    def _():
        h_ref[...] = jnp.zeros_like(h_ref)

    c = CHUNK
    n_pairs = RW_WIDTH // LANES
    strict, incl = _tri_masks(2 * c, c)
    row = lax.broadcasted_iota(jnp.int32, (c, c), 0)
    col = lax.broadcasted_iota(jnp.int32, (c, c), 1)
    cum_mat = (col <= row).astype(BF16)
    r2 = lax.broadcasted_iota(jnp.int32, (LANES, LANES), 0)
    c2 = lax.broadcasted_iota(jnp.int32, (LANES, LANES), 1)
    head_ones = (jnp.right_shift(r2, 6) == jnp.right_shift(c2, 6)).astype(BF16)
    eye = (r2 == c2).astype(F32)

    def chunk_body(ci, carry):
        rows = pl.ds(pl.multiple_of(ci * c, c), c)
        for p in range(n_pairs):
            lanes = slice(p * LANES, (p + 1) * LANES)
            r = r_ref[rows, lanes]
            k = k_ref[rows, lanes]
            v = v_ref[rows, lanes]
            lw = lw_ref[rows, lanes]
            eta = eta_ref[rows, lanes]

            kk = k * kk_ref[:, lanes]
            kk = kk / jnp.maximum(jnp.sqrt(_exact_dot_r(kk * kk, head_ones)), 1e-12)
            k2 = k * (1.0 + (eta - 1.0) * ka_ref[:, lanes])
            a_vec = -kk
            b_vec = kk * eta

            cum = _exact_dot(cum_mat, lw)
            cum_end = cum[c - 1:c, :]
            e_neg = jnp.exp(-cum)
            e_rel = jnp.exp(cum_end - cum)
            a_st = _stack_heads(a_vec * jnp.exp(cum - lw))
            r_st = _stack_heads(r * jnp.exp(cum))
            k_st = _stack_heads(k2 * e_neg)
            b_st = _stack_heads(b_vec * e_neg)
            kh_st = _stack_heads(k2 * e_rel)
            bh_st = _stack_heads(b_vec * e_rel)
            v_st = _stack_heads(v)

            big = _dot_nt(jnp.concatenate([a_st, r_st], axis=0),
                          jnp.concatenate([b_st, k_st], axis=0))
            a_ab = jnp.where(strict, big[:2 * c, :2 * c], 0.0)
            a_ak = jnp.where(strict, big[:2 * c, 2 * c:], 0.0)
            a_rb = jnp.where(incl, big[2 * c:, :2 * c], 0.0)
            a_rk = jnp.where(incl, big[2 * c:, 2 * c:], 0.0)

            m = a_ab
            inv = eye + a_ab
            for _ in range(int(math.log2(c)) - 1):
                m = _dot(m, m)
                inv = inv + _dot(inv, m)

            h0 = h_ref[p]
            u = _dot(inv, _dot(a_st, h0) + _dot(a_ak, v_st))
            y_st = _dot(r_st, h0) + _dot(a_rk, v_st) + _dot(a_rb, u)
            gam = jnp.exp(jnp.broadcast_to(cum_end, (LANES, LANES))).T
            h_ref[p] = gam * h0 + _dot(kh_st.T, v_st) + _dot(bh_st.T, u)
            y = y_st[:c] + y_st[c:]

            mean = _exact_dot_r(y, head_ones) / RW_HEAD_DIM
            yc = y - mean
            var = _exact_dot_r(yc * yc, head_ones) / RW_HEAD_DIM
            yn = yc * lax.rsqrt(var + RW_LN_EPS) * lnw_ref[:, lanes] + lnb_ref[:, lanes]
            bonus = _exact_dot_r(r * k2 * rk_ref[:, lanes], head_ones) * v
            o_ref[rows, lanes] = ((yn + bonus) * g_ref[rows, lanes]).astype(o_ref.dtype)
        return carry

    lax.fori_loop(0, r_ref.shape[0] // c, chunk_body, 0)


def _rwkv(batch, seq_len, r, k, v, lw, eta, g, k_k, k_a, r_k, ln_w, ln_b):
    n = r.shape[0]
    tb = TB_REC
    nt = seq_len // tb
    row = pl.BlockSpec((tb, RW_WIDTH), lambda b, t: (b * nt + t, 0))
    vecs = [p.reshape(1, RW_WIDTH) for p in (k_k, k_a, r_k, ln_w, ln_b)]
    return pl.pallas_call(
        _rwkv_kernel,
        grid=(batch, nt),
        in_specs=[row] * 6 + [_const_spec((1, RW_WIDTH))] * 5,
        out_specs=row,
        out_shape=jax.ShapeDtypeStruct((n, RW_WIDTH), BF16),
        scratch_shapes=[pltpu.VMEM((RW_WIDTH // LANES, LANES, LANES), F32)],
        compiler_params=_params("arbitrary", "arbitrary"),
        name="rwkv7",
    )(r, k, v, lw, eta, g, *vecs)


def _gla_kernel(q_ref, k_ref, v_ref, gg_ref, la_ref, nrm_ref, o_ref, s_ref):
    t = pl.program_id(1)

    @pl.when(t == 0)
    def _():
        s_ref[...] = jnp.zeros_like(s_ref)

    c = CHUNK
    n_pairs = GLA_KW // LANES
    row = lax.broadcasted_iota(jnp.int32, (c, c), 0)
    col = lax.broadcasted_iota(jnp.int32, (c, c), 1)
    cum_mat = (col <= row).astype(BF16)
    row2 = lax.broadcasted_iota(jnp.int32, (2 * c, c), 0)
    col2 = lax.broadcasted_iota(jnp.int32, (2 * c, c), 1)
    causal = col2 <= jnp.bitwise_and(row2, c - 1)
    scale = GLA_DK ** -0.5

    def chunk_body(ci, carry):
        rows = pl.ds(pl.multiple_of(ci * c, c), c)
        for p in range(n_pairs):
            lanes = slice(p * LANES, (p + 1) * LANES)
            q = q_ref[rows, lanes] * scale
            k = k_ref[rows, lanes]
            la = la_ref[rows, lanes]
            cum = _exact_dot(cum_mat, la)
            cum_end = cum[c - 1:c, :]
            q_st = _stack_heads(q * jnp.exp(cum))
            k_dec = k * jnp.exp(-cum)
            kh_st = _stack_heads(k * jnp.exp(cum_end - cum))
            scores = jnp.where(causal, _dot_nt(q_st, k_dec), 0.0)
            s0 = s_ref[p]
            inter = _dot(q_st, s0)
            gam = jnp.exp(jnp.broadcast_to(cum_end, (LANES, LANES))).T
            v_pair = v_ref[rows, 2 * p * GLA_DV:(2 * p + 2) * GLA_DV]
            v_st = jnp.concatenate([v_pair[:, :GLA_DV], v_pair[:, GLA_DV:]], axis=0)
            s_ref[p] = gam * s0 + _dot(kh_st.T, v_st)
            for hh in range(2):
                head = 2 * p + hh
                vl = slice(head * GLA_DV, (head + 1) * GLA_DV)
                hr = slice(hh * c, (hh + 1) * c)
                o = _dot(scores[hr], v_st[hr]) + inter[hr]
                o = o * lax.rsqrt(jnp.mean(o * o, axis=-1, keepdims=True) + GLA_NORM_EPS)
                o = o * nrm_ref[:, vl]
                gg = gg_ref[rows, vl]
                o_ref[rows, vl] = (o * (gg * _sigmoid(gg))).astype(o_ref.dtype)
        return carry

    lax.fori_loop(0, q_ref.shape[0] // c, chunk_body, 0)


def _gla(batch, seq_len, gq, gk, gv, gg, la, gla_norm):
    n = gq.shape[0]
    tb = TB_REC
    nt = seq_len // tb
    rowk = pl.BlockSpec((tb, GLA_KW), lambda b, t: (b * nt + t, 0))
    rowv = pl.BlockSpec((tb, GLA_VW), lambda b, t: (b * nt + t, 0))
    return pl.pallas_call(
        _gla_kernel,
        grid=(batch, nt),
        in_specs=[rowk, rowk, rowv, rowv, rowk, _const_spec((1, GLA_VW))],
        out_specs=rowv,
        out_shape=jax.ShapeDtypeStruct((n, GLA_VW), BF16),
        scratch_shapes=[pltpu.VMEM((GLA_KW // LANES, LANES, GLA_DV), F32)],
        compiler_params=_params("arbitrary", "arbitrary"),
        name="gla",
    )(gq, gk, gv, gg, la, gla_norm.reshape(1, GLA_VW))


def _odd_in_kernel(x_ref, gn_ref, w_ref, q_ref, k_ref, vt_ref):
    h = _rms_norm(x_ref[...], gn_ref[...], NORM_EPS)
    qkv = _dot(h, w_ref[...])
    d = q_ref.shape[1]
    q_ref[...] = (qkv[:, :d] * (DA_QK_DIM ** -0.5)).astype(q_ref.dtype)
    k_ref[...] = qkv[:, d:2 * d].astype(k_ref.dtype)
    vt_ref[0, 0] = qkv[:, 2 * d:].T.astype(vt_ref.dtype)


def _odd_in(x2, seq_len, gn, w_qkv):
    n, d = x2.shape
    tm = T_ATT
    nt = seq_len // tm
    row = pl.BlockSpec((tm, d), lambda i: (i, 0))
    return pl.pallas_call(
        _odd_in_kernel,
        grid=(n // tm,),
        in_specs=[row, _const_spec((1, d)), _const_spec(w_qkv.shape)],
        out_specs=[row, row, pl.BlockSpec((1, 1, d, tm), lambda i: (i // nt, i % nt, 0, 0))],
        out_shape=[jax.ShapeDtypeStruct((n, d), BF16)] * 2
        + [jax.ShapeDtypeStruct((n // seq_len, nt, d, tm), BF16)],
        compiler_params=_params("arbitrary"),
        name="odd_in",
    )(x2, gn.reshape(1, d), w_qkv.astype(BF16))


def _attn_kernel(lam_ref, q_ref, k_ref, vt_ref, bias_ref, subln_ref, o_ref,
                 m_ref, l_ref, acc_ref, s_ref, p_ref, alpha_ref, *, lam_init):
    i = pl.program_id(2)
    tq = q_ref.shape[0]
    tk = tq

    q_st = _stack_heads(q_ref[...].astype(F32)).astype(BF16)
    m_ref[...] = jnp.full_like(m_ref, -jnp.inf)
    l_ref[...] = jnp.zeros_like(l_ref)
    acc_ref[...] = jnp.zeros_like(acc_ref)

    def scores(j):
        rows = pl.ds(pl.multiple_of(j * tk, tk), tk)
        return _dot_nt(k_ref[rows, :], q_st)

    def accumulate(j, slot):
        acc_ref[...] = alpha_ref[slot] * acc_ref[...] + jnp.dot(
            vt_ref[0, j], p_ref[slot], preferred_element_type=F32)

    s_ref[0] = scores(0)
    p_ref[1] = jnp.zeros(p_ref.shape[1:], p_ref.dtype)
    alpha_ref[1] = jnp.ones(alpha_ref.shape[1:], F32)

    def stage(j, cur, nxt):
        s_ref[nxt] = scores(jnp.minimum(j + 1, i))
        accumulate(jnp.maximum(j - 1, 0), nxt)
        s = s_ref[cur] + bias_ref[0, jnp.minimum(i - j, 2)]
        m_prev = m_ref[...]
        m_new = jnp.maximum(m_prev, jnp.max(s, axis=0, keepdims=True))
        p = jnp.exp(s - m_new)
        alpha = jnp.exp(m_prev - m_new)
        alpha_ref[cur] = alpha
        l_ref[...] = alpha * l_ref[...] + jnp.sum(p, axis=0, keepdims=True)
        p_ref[cur] = p.astype(p_ref.dtype)
        m_ref[...] = m_new

    def body(jj, carry):
        j = 2 * jj
        stage(j, 0, 1)

        @pl.when(j + 1 <= i)
        def _():
            stage(j + 1, 1, 0)

        return carry

    lax.fori_loop(0, (i + 2) // 2, body, 0)
    for slot in range(2):
        @pl.when(lax.rem(i, 2) == slot)
        def _():
            accumulate(i, slot)

    lam_v = lam_ref[...]
    lam = (jnp.exp(jnp.sum(lam_v[0:1] * lam_v[1:2], axis=1, keepdims=True))
           - jnp.exp(jnp.sum(lam_v[2:3] * lam_v[3:4], axis=1, keepdims=True)) + lam_init)
    o_all = acc_ref[...] / l_ref[...]
    o = o_all[:, :tq] - lam * o_all[:, tq:]
    o = o * lax.rsqrt(jnp.mean(o * o, axis=0, keepdims=True) + DA_SUBLN_EPS)
    o = o * subln_ref[...] * (1.0 - lam_init)
    o_ref[...] = o.T.astype(o_ref.dtype)


def _t5_causal_buckets(dist):
    max_exact = REL_BUCKETS // 2
    ratio = jnp.maximum(dist, max_exact).astype(F32) / max_exact
    large = max_exact + (jnp.log(ratio) / math.log(REL_MAX_DIST / max_exact)
                         * (REL_BUCKETS - max_exact)).astype(jnp.int32)
    large = jnp.minimum(large, REL_BUCKETS - 1)
    return jnp.where(dist < max_exact, dist, large)


def _attn_bias_tiles(rel_bias, seq_len, tq):
    bias_by_dist = rel_bias[_t5_causal_buckets(jnp.arange(seq_len, dtype=jnp.int32))]
    kj = jnp.arange(tq, dtype=jnp.int32)[:, None]
    qi = jnp.arange(tq, dtype=jnp.int32)[None, :]
    diag = jnp.where((kj <= qi)[..., None], bias_by_dist[jnp.clip(qi - kj, 0, seq_len - 1)], NEG_INF)
    prev = bias_by_dist[tq + qi - kj]
    far = jnp.broadcast_to(rel_bias[REL_BUCKETS - 1], prev.shape)
    tiles = jnp.stack([diag, prev, far], axis=0).transpose(3, 0, 1, 2)
    return jnp.concatenate([tiles, tiles], axis=3).astype(F32)


def _diff_attn(batch, seq_len, q, k, vt, lam_params, subln, bias_tiles, lam_init):
    n, d = q.shape
    heads = d // DA_V_DIM
    tq = T_ATT
    nq = seq_len // tq
    assert tq > REL_MAX_DIST
    qspec = pl.BlockSpec((tq, DA_V_DIM), lambda b, h, i: (b * nq + i, h))
    subln_cols = jnp.broadcast_to(subln.reshape(DA_V_DIM, 1), (DA_V_DIM, tq))
    return pl.pallas_call(
        functools.partial(_attn_kernel, lam_init=lam_init),
        grid=(batch, heads, nq),
        in_specs=[_const_spec(lam_params.shape), qspec,
                  pl.BlockSpec((seq_len, DA_V_DIM), lambda b, h, i: (b, h)),
                  pl.BlockSpec((1, nq, DA_V_DIM, tq), lambda b, h, i: (b, 0, h, 0)),
                  pl.BlockSpec((1, 3, tq, 2 * tq), lambda b, h, i: (h, 0, 0, 0)),
                  _const_spec((DA_V_DIM, tq))],
        out_specs=qspec,
        out_shape=jax.ShapeDtypeStruct((n, d), BF16),
        scratch_shapes=[pltpu.VMEM((1, 2 * tq), F32), pltpu.VMEM((1, 2 * tq), F32),
                        pltpu.VMEM((DA_V_DIM, 2 * tq), F32),
                        pltpu.VMEM((2, tq, 2 * tq), F32), pltpu.VMEM((2, tq, 2 * tq), BF16),
                        pltpu.VMEM((2, 1, 2 * tq), F32)],
        compiler_params=_params("arbitrary", "arbitrary", "arbitrary"),
        name="diff_attn",
    )(lam_params, q, k, vt, bias_tiles, subln_cols)


def _ffn_kernel(*refs, n_mix, final):
    x_ref = refs[0]
    mix_refs = refs[1:1 + n_mix]
    wout_ref, gn_ref, wg_ref, wu_ref, wd_ref = refs[1 + n_mix:6 + n_mix]
    gfin_ref = refs[6 + n_mix] if final else None
    o_ref = refs[-1]

    mix = jnp.concatenate([m_ref[...] for m_ref in mix_refs], axis=1)
    x = x_ref[...] + jnp.dot(mix, wout_ref[...], preferred_element_type=F32)
    h = _rms_norm(x, gn_ref[...], NORM_EPS).astype(BF16)
    acc = x
    hidden = wg_ref.shape[1]
    for c0 in range(0, hidden, FFN_HC):
        cols = slice(c0, c0 + FFN_HC)
        gate = jnp.dot(h, wg_ref[:, cols], preferred_element_type=F32)
        up = jnp.dot(h, wu_ref[:, cols], preferred_element_type=F32)
        act = (gate * _sigmoid(gate)) * up
        acc = acc + jnp.dot(act.astype(BF16), wd_ref[cols, :], preferred_element_type=F32)
    if final:
        acc = _rms_norm(acc, gfin_ref[...], NORM_EPS)
    o_ref[...] = acc


def _ffn(x2, mixes, w_out, gn, wg, wu, wd, g_final=None):
    n, d = x2.shape
    tm = TM_FFN
    final = g_final is not None
    row = lambda w: pl.BlockSpec((tm, w), lambda i: (i, 0))
    consts = [w_out.astype(BF16), gn.reshape(1, d), wg.astype(BF16), wu.astype(BF16),
              wd.astype(BF16)]
    if final:
        consts.append(g_final.reshape(1, d))
    return pl.pallas_call(
        functools.partial(_ffn_kernel, n_mix=len(mixes), final=final),
        grid=(n // tm,),
        in_specs=([row(d)] + [row(m.shape[1]) for m in mixes]
                  + [_const_spec(c.shape) for c in consts]),
        out_specs=row(d),
        out_shape=jax.ShapeDtypeStruct((n, d), F32),
        compiler_params=_params("arbitrary"),
        name="ffn",
    )(x2, *mixes, *consts)


def _diff_lambda_init(layer):
    return 0.8 - 0.6 * math.exp(-0.3 * layer)


def kernel(x, rel_bias, norm_mix, norm_ffn, norm_final, ab_w_in, ab_w_out, rw_mu_rkv, rw_mu_wag, rw_w0, rw_w1, rw_w2, rw_a0, rw_a1, rw_a2, rw_g1, rw_g2, rw_k_k, rw_k_a, rw_r_k, rw_ln_w, rw_ln_b, gla_wa1, gla_wa2, gla_ba, gla_norm, da_w_qkv, da_w_out, da_lam_q1, da_lam_k1, da_lam_q2, da_lam_k2, da_subln, ffn_w_gate, ffn_w_up, ffn_w_down):
    batch, seq_len, d = x.shape
    depth = norm_mix.shape[0]
    x2 = x.reshape(batch * seq_len, d)
    bias_tiles = _attn_bias_tiles(rel_bias, seq_len, T_ATT)
    for layer in range(depth):
        i = layer // 2
        if layer % 2 == 0:
            (r, k, v, lw, eta, g, gq, gk, gv, gg, la) = _even_in(
                x2, seq_len, norm_mix[layer], ab_w_in[i], rw_mu_rkv[i], rw_mu_wag[i],
                rw_w0[i], rw_w1[i], rw_w2[i], rw_a0[i], rw_a1[i], rw_a2[i], rw_g1[i], rw_g2[i],
                gla_wa1[i], gla_wa2[i], gla_ba[i])
            o_a = _rwkv(batch, seq_len, r, k, v, lw, eta, g, rw_k_k[i], rw_k_a[i],
                        rw_r_k[i], rw_ln_w[i], rw_ln_b[i])
            o_b = _gla(batch, seq_len, gq, gk, gv, gg, la, gla_norm[i])
            mixes = [o_a, o_b]
            w_out = ab_w_out[i]
        else:
            q, k, vt = _odd_in(x2, seq_len, norm_mix[layer], da_w_qkv[i])
            lam_params = jnp.stack([da_lam_q1[i], da_lam_k1[i], da_lam_q2[i], da_lam_k2[i]])
            o = _diff_attn(batch, seq_len, q, k, vt, lam_params, da_subln[i], bias_tiles,
                           _diff_lambda_init(layer))
            mixes = [o]
            w_out = da_w_out[i]
        x2 = _ffn(x2, mixes, w_out, norm_ffn[layer], ffn_w_gate[layer], ffn_w_up[layer],
                  ffn_w_down[layer], norm_final if layer == depth - 1 else None)
    return x2.reshape(batch, seq_len, d)
```

```python
import functools
import math

import jax
import jax.numpy as jnp
from jax import lax
from jax.experimental import pallas as pl
from jax.experimental.pallas import tpu as pltpu

F32 = jnp.float32
BF16 = jnp.bfloat16

RW_HEAD_DIM = 64
RW_WIDTH = 512
RW_LN_EPS = 64e-5
GLA_HEADS = 4
GLA_DK = 64
GLA_DV = 128
GLA_KW = 256
GLA_VW = 512
GLA_GATE_NORMALIZER = 16.0
GLA_NORM_EPS = 1e-5
DA_QK_DIM = 64
DA_V_DIM = 128
DA_SUBLN_EPS = 1e-5
NEG_INF = -1e30
REL_BUCKETS = 32
REL_MAX_DIST = 128
NORM_EPS = 1e-6
LOG2E = math.log2(math.e)

LANES = 128
BF16_ROWS = 16
CHUNK = 64
VMEM_LIMIT = 56 * 1024 * 1024

TM_PROJ = 256
TM_FFN = 512
FFN_HC = 256
TB_REC = 256
RW_PREP = 2
T_ATT = 256


def _dot(a, b):
    return jnp.dot(a.astype(BF16), b.astype(BF16), preferred_element_type=F32)


def _dot_nt(a, b):
    return lax.dot_general(a.astype(BF16), b.astype(BF16),
                           (((1,), (1,)), ((), ())), preferred_element_type=F32)


def _split3(x):
    hi = x.astype(BF16)
    r1 = x - hi.astype(F32)
    mid = r1.astype(BF16)
    lo = (r1 - mid.astype(F32)).astype(BF16)
    return hi, mid, lo


def _exact_dot(m_bf16, x):
    w = x.shape[-1]
    parts = jnp.concatenate(_split3(x), axis=-1)
    y = jnp.dot(m_bf16, parts, preferred_element_type=F32)
    return y[:, :w] + y[:, w:2 * w] + y[:, 2 * w:]


def _exact_dot_r(x, m_bf16):
    hi, mid, lo = _split3(x)
    parts = jnp.concatenate([hi, mid, lo], axis=0)
    y = jnp.dot(parts, m_bf16, preferred_element_type=F32)
    n = x.shape[0]
    return y[:n] + y[n:2 * n] + y[2 * n:]


def _rms_norm(x, g, eps):
    y = x * lax.rsqrt(jnp.mean(x * x, axis=-1, keepdims=True) + eps)
    return y * g


def _softplus(z):
    return jnp.maximum(z, 0.0) + jnp.log(1.0 + jnp.exp(-jnp.abs(z)))


def _sigmoid(z):
    return 1.0 / (1.0 + jnp.exp(-z))


def _const_spec(shape):
    nd = len(shape)
    return pl.BlockSpec(shape, lambda *_: (0,) * nd, pipeline_mode=pl.Buffered(1))


def _params(*sem):
    return pltpu.CompilerParams(dimension_semantics=sem, vmem_limit_bytes=VMEM_LIMIT)


def _shift_rows(cur, prev_row):
    rolled = pltpu.roll(cur, shift=1, axis=0)
    row = lax.broadcasted_iota(jnp.int32, cur.shape, 0)
    return jnp.where(row == 0, prev_row, rolled)


def _even_in_kernel(x_ref, gn_ref, win_ref, mu_wag_ref, mu_rkv_ref,
                    w1_ref, w2_ref, w0_ref, a1_ref, a2_ref, a0_ref,
                    g1_ref, g2_ref, wa2_ref, ba_ref,
                    r_ref, k_ref, v_ref, lw_ref, eta_ref, g_ref,
                    gq_ref, gk_ref, gv_ref, gg_ref, la_ref,
                    carry_h, carry_p, *, tiles_per_seq):
    tm = x_ref.shape[0]

    @pl.when(pl.program_id(0) % tiles_per_seq == 0)
    def _():
        carry_h[...] = jnp.zeros_like(carry_h)
        carry_p[...] = jnp.zeros_like(carry_p)

    h = _rms_norm(x_ref[...], gn_ref[...], NORM_EPS)
    h_prev = _shift_rows(h, carry_h[0:1, :])
    carry_h[0:1, :] = h[tm - 1:tm, :]
    dh = h_prev - h
    xw = h + dh * mu_wag_ref[0:1, :]
    xa = h + dh * mu_wag_ref[1:2, :]
    xg = h + dh * mu_wag_ref[2:3, :]

    proj = _dot(h, win_ref[...])
    p_rkv = proj[:, :3 * RW_WIDTH]
    p_prev = _shift_rows(p_rkv, carry_p[0:1, :])
    carry_p[0:1, :] = p_rkv[tm - 1:tm, :]
    rkv = p_rkv + (p_prev - p_rkv) * mu_rkv_ref[...]
    r_ref[...] = rkv[:, :RW_WIDTH]
    k_ref[...] = rkv[:, RW_WIDTH:2 * RW_WIDTH]
    v_ref[...] = rkv[:, 2 * RW_WIDTH:]

    w_lin = w0_ref[...] + _dot(jnp.tanh(_dot(xw, w1_ref[...])), w2_ref[...])
    w_raw = -_softplus(-w_lin) - 0.5
    lw_ref[...] = -jnp.exp(w_raw)
    eta_ref[...] = _sigmoid(a0_ref[...] + _dot(_dot(xa, a1_ref[...]), a2_ref[...]))
    g_ref[...] = _dot(_sigmoid(_dot(xg, g1_ref[...])), g2_ref[...])

    o = 3 * RW_WIDTH
    gq_ref[...] = proj[:, o:o + GLA_KW]
    gk_ref[...] = proj[:, o + GLA_KW:o + 2 * GLA_KW]
    gv_ref[...] = proj[:, o + 2 * GLA_KW:o + 2 * GLA_KW + GLA_VW]
    gg_ref[...] = proj[:, o + 2 * GLA_KW + GLA_VW:o + 2 * GLA_KW + 2 * GLA_VW]
    lora = proj[:, o + 2 * GLA_KW + 2 * GLA_VW:]
    la_lin = _dot(lora, wa2_ref[...]) + ba_ref[...]
    la_ref[...] = -_softplus(-la_lin) / GLA_GATE_NORMALIZER


def _pad_cols(w, n):
    return jnp.pad(w, ((0, 0), (0, n - w.shape[1])))


def _pad_rows(w, n):
    return jnp.pad(w, ((0, n - w.shape[0]), (0, 0)))


def _even_in(x2, seq_len, gn, w_in, mu_rkv, mu_wag, w0, w1, w2, a0, a1, a2, g1, g2,
             wa1, wa2, ba):
    n, d = x2.shape
    tm = TM_PROJ
    win = jnp.concatenate([w_in, _pad_cols(wa1, LANES)], axis=1).astype(BF16)
    w1p, w2p = _pad_cols(w1, LANES).astype(BF16), _pad_rows(w2, LANES).astype(BF16)
    a1p, a2p = _pad_cols(a1, LANES).astype(BF16), _pad_rows(a2, LANES).astype(BF16)
    g1p, g2p = _pad_cols(g1, 2 * LANES).astype(BF16), _pad_rows(g2, 2 * LANES).astype(BF16)
    wa2p = _pad_rows(wa2, LANES).astype(BF16)
    consts = [gn.reshape(1, d), win, mu_wag, mu_rkv.reshape(1, -1),
              w1p, w2p, w0.reshape(1, -1), a1p, a2p, a0.reshape(1, -1),
              g1p, g2p, wa2p, ba.reshape(1, -1)]
    widths = [RW_WIDTH] * 6 + [GLA_KW, GLA_KW, GLA_VW, GLA_VW, GLA_KW]
    row = lambda w: pl.BlockSpec((tm, w), lambda i: (i, 0))
    return pl.pallas_call(
        functools.partial(_even_in_kernel, tiles_per_seq=seq_len // tm),
        grid=(n // tm,),
        in_specs=[row(d)] + [_const_spec(c.shape) for c in consts],
        out_specs=[row(w) for w in widths],
        out_shape=[jax.ShapeDtypeStruct((n, w), F32) for w in widths],
        scratch_shapes=[pltpu.VMEM((8, d), F32), pltpu.VMEM((8, 3 * RW_WIDTH), F32)],
        compiler_params=_params("arbitrary"),
        name="even_in",
    )(x2, *consts)


def _lane_half_masks(shape):
    lane = lax.broadcasted_iota(jnp.int32, shape, len(shape) - 1)
    return lane < 64, lane >= 64


def _stack_heads(x):
    m0, m1 = _lane_half_masks(x.shape)
    return jnp.concatenate([jnp.where(m0, x, 0.0), jnp.where(m1, x, 0.0)], axis=0)


def _tri_masks(n, blk):
    row = lax.broadcasted_iota(jnp.int32, (n, n), 0)
    col = lax.broadcasted_iota(jnp.int32, (n, n), 1)
    shift = int(math.log2(blk))
    same = jnp.right_shift(row, shift) == jnp.right_shift(col, shift)
    return same & (col < row), same & (col <= row)


def _rwkv_kernel(r_ref, k_ref, v_ref, lw_ref, eta_ref, g_ref,
                 kk_ref, ka_ref, rk_ref, lnw_ref, lnb_ref, o_ref,
                 h_ref, rg_ref, hadd_ref, y_ref, gam_ref):
    t = pl.program_id(1)

    @pl.when(t == 0)
    def _():
        h_ref[...] = jnp.zeros_like(h_ref)

    c = CHUNK
    c2 = 2 * c
    n_pairs = RW_WIDTH // LANES
    n_chunks = r_ref.shape[0] // c
    strict, incl = _tri_masks(c2, c)
    row = lax.broadcasted_iota(jnp.int32, (c, c), 0)
    col = lax.broadcasted_iota(jnp.int32, (c, c), 1)
    cum_mat = (col <= row).astype(BF16)
    ri = lax.broadcasted_iota(jnp.int32, (LANES, LANES), 0)
    ci_ = lax.broadcasted_iota(jnp.int32, (LANES, LANES), 1)
    head_ones = (jnp.right_shift(ri, 6) == jnp.right_shift(ci_, 6)).astype(BF16)
    eye = (ri == ci_).astype(F32)

    def chunk_rows(ci):
        return pl.ds(pl.multiple_of(ci * c, c), c)

    def lanes_of(p):
        return slice(p * LANES, (p + 1) * LANES)

    def k_terms(rows, p):
        k = k_ref[rows, lanes_of(p)]
        eta = eta_ref[rows, lanes_of(p)]
        return k, eta, k * (1.0 + (eta - 1.0) * ka_ref[:, lanes_of(p)])

    def prep(grp, carry):
        probs = [(grp * RW_PREP + cc, p) for cc in range(RW_PREP) for p in range(n_pairs)]
        each = lambda f: [f(ci, chunk_rows(ci), p) for ci, p in probs]
        zipped = lambda f, *ls: [f(*xs) for xs in zip(*ls)]

        kt = each(lambda ci, rows, p: k_terms(rows, p))
        k2 = [x[2] for x in kt]
        eta = [x[1] for x in kt]
        kk0 = each(lambda ci, rows, p: k_ref[rows, lanes_of(p)] * kk_ref[:, lanes_of(p)])
        ss = [_exact_dot_r(x * x, head_ones) for x in kk0]
        kk = zipped(lambda x, s: x / jnp.maximum(jnp.sqrt(s), 1e-12), kk0, ss)
        lw = each(lambda ci, rows, p: lw_ref[rows, lanes_of(p)])
        cum = [_exact_dot(cum_mat, x) for x in lw]
        cum_end = [x[c - 1:c, :] for x in cum]
        e_neg = [jnp.exp(-x) for x in cum]
        e_rel = zipped(lambda x, e: jnp.exp(e - x), cum, cum_end)
        b_vec = zipped(lambda x, e: x * e, kk, eta)
        a_st = zipped(lambda x, cm, l: _stack_heads(-x * jnp.exp(cm - l)), kk, cum, lw)
        r_st = each(lambda ci, rows, p: r_ref[rows, lanes_of(p)])
        r_st = zipped(lambda x, cm: _stack_heads(x * jnp.exp(cm)), r_st, cum)
        k_st = zipped(lambda x, e: _stack_heads(x * e), k2, e_neg)
        b_st = zipped(lambda x, e: _stack_heads(x * e), b_vec, e_neg)
        kh_t = zipped(lambda x, e: _stack_heads(x * e).T, k2, e_rel)
        bh_t = zipped(lambda x, e: _stack_heads(x * e).T, b_vec, e_rel)
        v_st = each(lambda ci, rows, p: _stack_heads(v_ref[rows, lanes_of(p)]))

        big = zipped(lambda a, r, b, k: _dot_nt(jnp.concatenate([a, r], axis=0),
                                                jnp.concatenate([b, k], axis=0)),
                     a_st, r_st, b_st, k_st)
        a_ab = [jnp.where(strict, x[:c2, :c2], 0.0) for x in big]
        a_ak = [jnp.where(strict, x[:c2, c2:], 0.0) for x in big]
        a_rb = [jnp.where(incl, x[c2:, :c2], 0.0) for x in big]
        a_rk = [jnp.where(incl, x[c2:, c2:], 0.0) for x in big]

        x2 = zipped(_dot, a_ak, v_st)
        m = zipped(_dot, a_ab, a_ab)
        tinv = [eye + x for x in a_ab]
        for _ in range(int(math.log2(c)) - 2):
            prod = zipped(lambda mm, pp: _dot(mm, jnp.concatenate([pp, mm], axis=1)), m, tinv)
            tinv = zipped(lambda pp, pr: pp + pr[:, :c2], tinv, prod)
            m = [pr[:, c2:] for pr in prod]
        tinv = zipped(lambda pp, mm: pp + _dot(mm, pp), tinv, m)

        wu = zipped(lambda ti, a, x: _dot(ti, jnp.concatenate([a, x], axis=1)), tinv, a_st, x2)
        w = [x[:, :LANES] for x in wu]
        vu = zipped(lambda vv, x: jnp.concatenate([vv, x[:, LANES:]], axis=0), v_st, wu)
        y0 = zipped(lambda rk, rb, x: _dot(jnp.concatenate([rk, rb], axis=1), x), a_rk, a_rb, vu)
        hadd = zipped(lambda kt_, bt_, x: _dot(jnp.concatenate([kt_, bt_], axis=1), x), kh_t, bh_t, vu)
        gr = zipped(lambda rb, bt_, ww: _dot(jnp.concatenate([rb, bt_], axis=0), ww), a_rb, bh_t, w)
        for i, (ci, p) in enumerate(probs):
            rg_ref[ci, p, :c2] = (r_st[i] + gr[i][:c2]).astype(rg_ref.dtype)
            rg_ref[ci, p, c2:] = gr[i][c2:].astype(rg_ref.dtype)
            hadd_ref[ci, p] = hadd[i]
            y_ref[ci, p] = y0[i]
            gam_ref[ci, p] = jnp.exp(jnp.broadcast_to(cum_end[i], (LANES, LANES))).T
        return carry

    lax.fori_loop(0, n_chunks // RW_PREP, prep, 0)

    def advance(ci, carry):
        h0 = [h_ref[p] for p in range(n_pairs)]
        prod = [_dot(rg_ref[ci, p], h0[p]) for p in range(n_pairs)]
        for p in range(n_pairs):
            y_ref[ci, p] = y_ref[ci, p] + prod[p][:c2]
            h_ref[p] = gam_ref[ci, p] * h0[p] + prod[p][c2:] + hadd_ref[ci, p]
        return carry

    lax.fori_loop(0, n_chunks, advance, 0)

    def finish(ci, carry):
        rows = chunk_rows(ci)
        ys = [y_ref[ci, p] for p in range(n_pairs)]
        y = [x[:c] + x[c:] for x in ys]
        mean = [_exact_dot_r(x, head_ones) / RW_HEAD_DIM for x in y]
        yc = [x - mu for x, mu in zip(y, mean)]
        var = [_exact_dot_r(x * x, head_ones) / RW_HEAD_DIM for x in yc]
        rk2 = [r_ref[rows, lanes_of(p)] * k_terms(rows, p)[2] * rk_ref[:, lanes_of(p)]
               for p in range(n_pairs)]
        bsum = [_exact_dot_r(x, head_ones) for x in rk2]
        for p in range(n_pairs):
            lanes = lanes_of(p)
            yn = yc[p] * lax.rsqrt(var[p] + RW_LN_EPS) * lnw_ref[:, lanes] + lnb_ref[:, lanes]
            out = (yn + bsum[p] * v_ref[rows, lanes]) * g_ref[rows, lanes]
            o_ref[rows, lanes] = out.astype(o_ref.dtype)
        return carry

    lax.fori_loop(0, n_chunks, finish, 0)


def _rwkv(batch, seq_len, r, k, v, lw, eta, g, k_k, k_a, r_k, ln_w, ln_b):
    n = r.shape[0]
    tb = TB_REC
    nt = seq_len // tb
    n_pairs = RW_WIDTH // LANES
    per_chunk = (tb // CHUNK, n_pairs)
    row = pl.BlockSpec((tb, RW_WIDTH), lambda b, t: (b * nt + t, 0))
    vecs = [p.reshape(1, RW_WIDTH) for p in (k_k, k_a, r_k, ln_w, ln_b)]
    return pl.pallas_call(
        _rwkv_kernel,
        grid=(batch, nt),
        in_specs=[row] * 6 + [_const_spec((1, RW_WIDTH))] * 5,
        out_specs=row,
        out_shape=jax.ShapeDtypeStruct((n, RW_WIDTH), BF16),
        scratch_shapes=[pltpu.VMEM((n_pairs, LANES, LANES), F32),
                        pltpu.VMEM(per_chunk + (2 * LANES, LANES), BF16),
                        pltpu.VMEM(per_chunk + (LANES, LANES), F32),
                        pltpu.VMEM(per_chunk + (LANES, LANES), F32),
                        pltpu.VMEM(per_chunk + (LANES, LANES), F32)],
        compiler_params=_params("arbitrary", "arbitrary"),
        name="rwkv7",
    )(r, k, v, lw, eta, g, *vecs)


def _gla_kernel(q_ref, k_ref, v_ref, gg_ref, la_ref, nrm_ref, o_ref, s_ref):
    t = pl.program_id(1)

    @pl.when(t == 0)
    def _():
        s_ref[...] = jnp.zeros_like(s_ref)

    c = CHUNK
    n_pairs = GLA_KW // LANES
    row = lax.broadcasted_iota(jnp.int32, (c, c), 0)
    col = lax.broadcasted_iota(jnp.int32, (c, c), 1)
    cum_mat = (col <= row).astype(BF16)
    row2 = lax.broadcasted_iota(jnp.int32, (2 * c, c), 0)
    col2 = lax.broadcasted_iota(jnp.int32, (2 * c, c), 1)
    causal = col2 <= jnp.bitwise_and(row2, c - 1)
    scale = GLA_DK ** -0.5

    def chunk_body(ci, carry):
        rows = pl.ds(pl.multiple_of(ci * c, c), c)
        for p in range(n_pairs):
            lanes = slice(p * LANES, (p + 1) * LANES)
            q = q_ref[rows, lanes] * scale
            k = k_ref[rows, lanes]
            la = la_ref[rows, lanes]
            cum = _exact_dot(cum_mat, la)
            cum_end = cum[c - 1:c, :]
            q_st = _stack_heads(q * jnp.exp(cum))
            k_dec = k * jnp.exp(-cum)
            kh_st = _stack_heads(k * jnp.exp(cum_end - cum))
            scores = jnp.where(causal, _dot_nt(q_st, k_dec), 0.0)
            s0 = s_ref[p]
            inter = _dot(q_st, s0)
            gam = jnp.exp(jnp.broadcast_to(cum_end, (LANES, LANES))).T
            v_pair = v_ref[rows, 2 * p * GLA_DV:(2 * p + 2) * GLA_DV]
            v_st = jnp.concatenate([v_pair[:, :GLA_DV], v_pair[:, GLA_DV:]], axis=0)
            s_ref[p] = gam * s0 + _dot(kh_st.T, v_st)
            for hh in range(2):
                head = 2 * p + hh
                vl = slice(head * GLA_DV, (head + 1) * GLA_DV)
                hr = slice(hh * c, (hh + 1) * c)
                o = _dot(scores[hr], v_st[hr]) + inter[hr]
                o = o * lax.rsqrt(jnp.mean(o * o, axis=-1, keepdims=True) + GLA_NORM_EPS)
                o = o * nrm_ref[:, vl]
                gg = gg_ref[rows, vl]
                o_ref[rows, vl] = (o * (gg * _sigmoid(gg))).astype(o_ref.dtype)
        return carry

    lax.fori_loop(0, q_ref.shape[0] // c, chunk_body, 0)


def _gla(batch, seq_len, gq, gk, gv, gg, la, gla_norm):
    n = gq.shape[0]
    tb = TB_REC
    nt = seq_len // tb
    rowk = pl.BlockSpec((tb, GLA_KW), lambda b, t: (b * nt + t, 0))
    rowv = pl.BlockSpec((tb, GLA_VW), lambda b, t: (b * nt + t, 0))
    return pl.pallas_call(
        _gla_kernel,
        grid=(batch, nt),
        in_specs=[rowk, rowk, rowv, rowv, rowk, _const_spec((1, GLA_VW))],
        out_specs=rowv,
        out_shape=jax.ShapeDtypeStruct((n, GLA_VW), BF16),
        scratch_shapes=[pltpu.VMEM((GLA_KW // LANES, LANES, GLA_DV), F32)],
        compiler_params=_params("arbitrary", "arbitrary"),
        name="gla",
    )(gq, gk, gv, gg, la, gla_norm.reshape(1, GLA_VW))


def _odd_in_kernel(x_ref, gn_ref, w_ref, q_ref, k_ref, vt_ref):
    h = _rms_norm(x_ref[...], gn_ref[...], NORM_EPS)
    qkv = _dot(h, w_ref[...])
    d = q_ref.shape[1]
    q_ref[...] = (qkv[:, :d] * (DA_QK_DIM ** -0.5 * LOG2E)).astype(q_ref.dtype)
    k_ref[...] = qkv[:, d:2 * d].astype(k_ref.dtype)
    vt_ref[0, 0] = qkv[:, 2 * d:].T.astype(vt_ref.dtype)


def _odd_in(x2, seq_len, gn, w_qkv):
    n, d = x2.shape
    tm = T_ATT
    nt = seq_len // tm
    row = pl.BlockSpec((tm, d), lambda i: (i, 0))
    return pl.pallas_call(
        _odd_in_kernel,
        grid=(n // tm,),
        in_specs=[row, _const_spec((1, d)), _const_spec(w_qkv.shape)],
        out_specs=[row, row, pl.BlockSpec((1, 1, d, tm), lambda i: (i // nt, i % nt, 0, 0))],
        out_shape=[jax.ShapeDtypeStruct((n, d), BF16)] * 2
        + [jax.ShapeDtypeStruct((n // seq_len, nt, d, tm), BF16)],
        compiler_params=_params("arbitrary"),
        name="odd_in",
    )(x2, gn.reshape(1, d), w_qkv.astype(BF16))


def _attn_kernel(lam_ref, q_ref, k_ref, vt_ref, bias_ref, subln_ref, o_ref,
                 m_ref, acc_ref, qst_ref, p_ref, alpha_ref, *, lam_init):
    i = pl.program_id(2)
    tq = q_ref.shape[0]
    tk = tq
    dv = vt_ref.shape[2]

    qst_ref[...] = _stack_heads(q_ref[...].astype(F32)).T.astype(BF16)
    m_ref[...] = jnp.full_like(m_ref, -jnp.inf)
    acc_ref[...] = jnp.zeros_like(acc_ref)
    ones_rows = jnp.ones((acc_ref.shape[0] - dv, tk), BF16)

    def accumulate(j, slot):
        lhs = jnp.concatenate([vt_ref[0, j], ones_rows], axis=0)
        acc_ref[...] = alpha_ref[slot] * acc_ref[...] + jnp.dot(
            lhs, p_ref[slot], preferred_element_type=F32)

    p_ref[1] = jnp.zeros(p_ref.shape[1:], p_ref.dtype)
    alpha_ref[1] = jnp.ones(alpha_ref.shape[1:], F32)

    def stage(j, cur, prev):
        k_blk = k_ref[pl.ds(pl.multiple_of(jnp.minimum(j, i) * tk, tk), tk), :]
        tile = jnp.where(j > i, 3, jnp.minimum(i - j, 2))
        for g in range(2 * tq // LANES):
            cols = slice(g * LANES, (g + 1) * LANES)
            s = jnp.dot(k_blk, qst_ref[:, cols], preferred_element_type=F32)
            s = s + bias_ref[0, tile, :, cols]
            m_prev = m_ref[:, cols]
            m_new = jnp.maximum(m_prev, jnp.max(s, axis=0, keepdims=True))
            p_ref[cur, :, cols] = jnp.exp2((s - m_new).astype(p_ref.dtype))
            alpha_ref[cur, :, cols] = jnp.exp2(m_prev - m_new)
            m_ref[:, cols] = m_new
        accumulate(jnp.maximum(j - 1, 0), prev)

    def body(jj, carry):
        stage(2 * jj, 0, 1)
        stage(2 * jj + 1, 1, 0)
        return carry

    n_trips = (i + 2) // 2
    lax.fori_loop(0, n_trips, body, 0)
    accumulate(jnp.minimum(2 * n_trips - 1, i), 1)

    lam_v = lam_ref[...]
    lam = (jnp.exp(jnp.sum(lam_v[0:1] * lam_v[1:2], axis=1, keepdims=True))
           - jnp.exp(jnp.sum(lam_v[2:3] * lam_v[3:4], axis=1, keepdims=True)) + lam_init)
    o_all = acc_ref[:dv, :] / acc_ref[dv:dv + 1, :]
    o = o_all[:, :tq] - lam * o_all[:, tq:]
    o = o * lax.rsqrt(jnp.mean(o * o, axis=0, keepdims=True) + DA_SUBLN_EPS)
    o = o * subln_ref[...] * (1.0 - lam_init)
    o_ref[...] = o.T.astype(o_ref.dtype)


def _t5_causal_buckets(dist):
    max_exact = REL_BUCKETS // 2
    ratio = jnp.maximum(dist, max_exact).astype(F32) / max_exact
    large = max_exact + (jnp.log(ratio) / math.log(REL_MAX_DIST / max_exact)
                         * (REL_BUCKETS - max_exact)).astype(jnp.int32)
    large = jnp.minimum(large, REL_BUCKETS - 1)
    return jnp.where(dist < max_exact, dist, large)


def _attn_bias_tiles(rel_bias, seq_len, tq):
    heads = rel_bias.shape[1]
    by_dist = rel_bias[_t5_causal_buckets(jnp.arange(2 * tq, dtype=jnp.int32))].T

    def toeplitz(vec):
        n = vec.shape[1] // 2
        flat = jnp.tile(vec, (1, n))[:, :n * (2 * n - 1)]
        return flat.reshape(heads, n, 2 * n - 1)[:, :, :n]

    diag = toeplitz(jnp.concatenate([by_dist[:, :tq], jnp.full((heads, tq), NEG_INF, F32)], axis=1))
    prev = toeplitz(jnp.concatenate([by_dist[:, tq:], by_dist[:, :tq]], axis=1))
    far = jnp.broadcast_to(rel_bias[REL_BUCKETS - 1][:, None, None], prev.shape)
    masked = jnp.full(prev.shape, NEG_INF, F32)
    tiles = jnp.stack([diag, prev, far, masked], axis=1)
    return (jnp.concatenate([tiles, tiles], axis=3) * LOG2E).astype(F32)


def _diff_attn(batch, seq_len, q, k, vt, lam_params, subln, bias_tiles, lam_init):
    n, d = q.shape
    heads = d // DA_V_DIM
    tq = T_ATT
    nq = seq_len // tq
    assert tq > REL_MAX_DIST
    qspec = pl.BlockSpec((tq, DA_V_DIM), lambda b, h, i: (b * nq + i, h))
    subln_cols = jnp.broadcast_to(subln.reshape(DA_V_DIM, 1), (DA_V_DIM, tq))
    return pl.pallas_call(
        functools.partial(_attn_kernel, lam_init=lam_init),
        grid=(batch, heads, nq),
        in_specs=[_const_spec(lam_params.shape), qspec,
                  pl.BlockSpec((seq_len, DA_V_DIM), lambda b, h, i: (b, h)),
                  pl.BlockSpec((1, nq, DA_V_DIM, tq), lambda b, h, i: (b, 0, h, 0)),
                  pl.BlockSpec((1, 4, tq, 2 * tq), lambda b, h, i: (h, 0, 0, 0)),
                  _const_spec((DA_V_DIM, tq))],
        out_specs=qspec,
        out_shape=jax.ShapeDtypeStruct((n, d), BF16),
        scratch_shapes=[pltpu.VMEM((1, 2 * tq), F32),
                        pltpu.VMEM((DA_V_DIM + BF16_ROWS, 2 * tq), F32),
                        pltpu.VMEM((DA_V_DIM, 2 * tq), BF16), pltpu.VMEM((2, tq, 2 * tq), BF16),
                        pltpu.VMEM((2, 1, 2 * tq), F32)],
        compiler_params=_params("arbitrary", "arbitrary", "arbitrary"),
        name="diff_attn",
    )(lam_params, q, k, vt, bias_tiles, subln_cols)


def _ffn_kernel(*refs, n_mix, final):
    x_ref = refs[0]
    mix_refs = refs[1:1 + n_mix]
    wout_ref, gn_ref, wg_ref, wu_ref, wd_ref = refs[1 + n_mix:6 + n_mix]
    gfin_ref = refs[6 + n_mix] if final else None
    o_ref = refs[-1]

    mix = jnp.concatenate([m_ref[...] for m_ref in mix_refs], axis=1)
    x = x_ref[...] + jnp.dot(mix, wout_ref[...], preferred_element_type=F32)
    h = _rms_norm(x, gn_ref[...], NORM_EPS).astype(BF16)
    acc = x
    hidden = wg_ref.shape[1]
    for c0 in range(0, hidden, FFN_HC):
        cols = slice(c0, c0 + FFN_HC)
        gate = jnp.dot(h, wg_ref[:, cols], preferred_element_type=F32)
        up = jnp.dot(h, wu_ref[:, cols], preferred_element_type=F32)
        act = (gate * _sigmoid(gate)) * up
        acc = acc + jnp.dot(act.astype(BF16), wd_ref[cols, :], preferred_element_type=F32)
    if final:
        acc = _rms_norm(acc, gfin_ref[...], NORM_EPS)
    o_ref[...] = acc


def _ffn(x2, mixes, w_out, gn, wg, wu, wd, g_final=None):
    n, d = x2.shape
    tm = TM_FFN
    final = g_final is not None
    row = lambda w: pl.BlockSpec((tm, w), lambda i: (i, 0))
    consts = [w_out.astype(BF16), gn.reshape(1, d), wg.astype(BF16), wu.astype(BF16),
              wd.astype(BF16)]
    if final:
        consts.append(g_final.reshape(1, d))
    return pl.pallas_call(
        functools.partial(_ffn_kernel, n_mix=len(mixes), final=final),
        grid=(n // tm,),
        in_specs=([row(d)] + [row(m.shape[1]) for m in mixes]
                  + [_const_spec(c.shape) for c in consts]),
        out_specs=row(d),
        out_shape=jax.ShapeDtypeStruct((n, d), F32),
        compiler_params=_params("arbitrary"),
        name="ffn",
    )(x2, *mixes, *consts)


def _diff_lambda_init(layer):
    return 0.8 - 0.6 * math.exp(-0.3 * layer)


def kernel(x, rel_bias, norm_mix, norm_ffn, norm_final, ab_w_in, ab_w_out, rw_mu_rkv, rw_mu_wag, rw_w0, rw_w1, rw_w2, rw_a0, rw_a1, rw_a2, rw_g1, rw_g2, rw_k_k, rw_k_a, rw_r_k, rw_ln_w, rw_ln_b, gla_wa1, gla_wa2, gla_ba, gla_norm, da_w_qkv, da_w_out, da_lam_q1, da_lam_k1, da_lam_q2, da_lam_k2, da_subln, ffn_w_gate, ffn_w_up, ffn_w_down):
    batch, seq_len, d = x.shape
    depth = norm_mix.shape[0]
    x2 = x.reshape(batch * seq_len, d)
    bias_tiles = _attn_bias_tiles(rel_bias, seq_len, T_ATT)
    for layer in range(depth):
        i = layer // 2
        if layer % 2 == 0:
            (r, k, v, lw, eta, g, gq, gk, gv, gg, la) = _even_in(
                x2, seq_len, norm_mix[layer], ab_w_in[i], rw_mu_rkv[i], rw_mu_wag[i],
                rw_w0[i], rw_w1[i], rw_w2[i], rw_a0[i], rw_a1[i], rw_a2[i], rw_g1[i], rw_g2[i],
                gla_wa1[i], gla_wa2[i], gla_ba[i])
            o_a = _rwkv(batch, seq_len, r, k, v, lw, eta, g, rw_k_k[i], rw_k_a[i],
                        rw_r_k[i], rw_ln_w[i], rw_ln_b[i])
            o_b = _gla(batch, seq_len, gq, gk, gv, gg, la, gla_norm[i])
            mixes = [o_a, o_b]
            w_out = ab_w_out[i]
        else:
            q, k, vt = _odd_in(x2, seq_len, norm_mix[layer], da_w_qkv[i])
            lam_params = jnp.stack([da_lam_q1[i], da_lam_k1[i], da_lam_q2[i], da_lam_k2[i]])
            o = _diff_attn(batch, seq_len, q, k, vt, lam_params, da_subln[i], bias_tiles,
                           _diff_lambda_init(layer))
            mixes = [o]
            w_out = da_w_out[i]
        x2 = _ffn(x2, mixes, w_out, norm_ffn[layer], ffn_w_gate[layer], ffn_w_up[layer],
                  ffn_w_down[layer], norm_final if layer == depth - 1 else None)
    return x2.reshape(batch, seq_len, d)
```

```python
import functools
import math

import jax
import jax.numpy as jnp
import numpy as np
from jax import lax
from jax.experimental import pallas as pl
from jax.experimental.pallas import tpu as pltpu

F32 = jnp.float32
BF16 = jnp.bfloat16

RW_HEAD_DIM = 64
RW_WIDTH = 512
RW_LN_EPS = 64e-5
GLA_HEADS = 4
GLA_DK = 64
GLA_DV = 128
GLA_KW = 256
GLA_VW = 512
GLA_GATE_NORMALIZER = 16.0
GLA_NORM_EPS = 1e-5
DA_QK_DIM = 64
DA_V_DIM = 128
DA_SUBLN_EPS = 1e-5
NEG_INF = -1e30
REL_BUCKETS = 32
REL_MAX_DIST = 128
NORM_EPS = 1e-6
LOG2E = math.log2(math.e)

LANES = 128
BF16_ROWS = 16
CHUNK = 64
VMEM_LIMIT = 56 * 1024 * 1024

TM_PROJ = 512
TM_FFN = 1024
FFN_HC = 256
TB_REC = 256
RW_PREP = 2
T_ATT = 256
TQ_ATT = 512
ATT_GROUP = 128


def _dot(a, b):
    return jnp.dot(a.astype(BF16), b.astype(BF16), preferred_element_type=F32)


def _dot_nt(a, b):
    return lax.dot_general(a.astype(BF16), b.astype(BF16),
                           (((1,), (1,)), ((), ())), preferred_element_type=F32)


def _split3(x):
    hi = x.astype(BF16)
    r1 = x - hi.astype(F32)
    mid = r1.astype(BF16)
    lo = (r1 - mid.astype(F32)).astype(BF16)
    return hi, mid, lo


def _exact_dot(m_bf16, x):
    w = x.shape[-1]
    parts = jnp.concatenate(_split3(x), axis=-1)
    y = jnp.dot(m_bf16, parts, preferred_element_type=F32)
    return y[:, :w] + y[:, w:2 * w] + y[:, 2 * w:]


def _exact_dot_r(x, m_bf16):
    hi, mid, lo = _split3(x)
    parts = jnp.concatenate([hi, mid, lo], axis=0)
    y = jnp.dot(parts, m_bf16, preferred_element_type=F32)
    n = x.shape[0]
    return y[:n] + y[n:2 * n] + y[2 * n:]


def _rms_norm(x, g, eps):
    y = x * lax.rsqrt(jnp.mean(x * x, axis=-1, keepdims=True) + eps)
    return y * g


def _softplus(z):
    return jnp.maximum(z, 0.0) + jnp.log(1.0 + jnp.exp(-jnp.abs(z)))


def _sigmoid(z):
    return 1.0 / (1.0 + jnp.exp(-z))


def _const_spec(shape):
    nd = len(shape)
    return pl.BlockSpec(shape, lambda *_: (0,) * nd, pipeline_mode=pl.Buffered(1))


def _params(*sem):
    return pltpu.CompilerParams(dimension_semantics=sem, vmem_limit_bytes=VMEM_LIMIT)


def _shift_rows(cur, prev_row):
    rolled = pltpu.roll(cur, shift=1, axis=0)
    row = lax.broadcasted_iota(jnp.int32, cur.shape, 0)
    return jnp.where(row == 0, prev_row, rolled)


def _even_in_kernel(x_ref, gn_ref, win_ref, mu_wag_ref, mu_rkv_ref,
                    w1_ref, w2_ref, w0_ref, a1_ref, a2_ref, a0_ref,
                    g1_ref, g2_ref, wa2_ref, ba_ref,
                    r_ref, k_ref, v_ref, lw_ref, eta_ref, g_ref,
                    gq_ref, gk_ref, gv_ref, gg_ref, la_ref,
                    carry_h, carry_p, *, tiles_per_seq):
    tm = x_ref.shape[0]

    @pl.when(pl.program_id(0) % tiles_per_seq == 0)
    def _():
        carry_h[...] = jnp.zeros_like(carry_h)
        carry_p[...] = jnp.zeros_like(carry_p)

    h = _rms_norm(x_ref[...], gn_ref[...], NORM_EPS)
    h_prev = _shift_rows(h, carry_h[0:1, :])
    carry_h[0:1, :] = h[tm - 1:tm, :]
    dh = h_prev - h
    xw = h + dh * mu_wag_ref[0:1, :]
    xa = h + dh * mu_wag_ref[1:2, :]
    xg = h + dh * mu_wag_ref[2:3, :]

    proj = _dot(h, win_ref[...])
    p_rkv = proj[:, :3 * RW_WIDTH]
    p_prev = _shift_rows(p_rkv, carry_p[0:1, :])
    carry_p[0:1, :] = p_rkv[tm - 1:tm, :]
    rkv = p_rkv + (p_prev - p_rkv) * mu_rkv_ref[...]
    r_ref[...] = rkv[:, :RW_WIDTH].astype(r_ref.dtype)
    k_ref[...] = rkv[:, RW_WIDTH:2 * RW_WIDTH].astype(k_ref.dtype)
    v_ref[...] = rkv[:, 2 * RW_WIDTH:].astype(v_ref.dtype)

    w_lin = w0_ref[...] + _dot(jnp.tanh(_dot(xw, w1_ref[...])), w2_ref[...])
    w_raw = -_softplus(-w_lin) - 0.5
    lw_ref[...] = -jnp.exp(w_raw)
    eta_ref[...] = _sigmoid(a0_ref[...] + _dot(_dot(xa, a1_ref[...]), a2_ref[...]))
    g_ref[...] = _dot(_sigmoid(_dot(xg, g1_ref[...])), g2_ref[...]).astype(g_ref.dtype)

    o = 3 * RW_WIDTH
    gq_ref[...] = proj[:, o:o + GLA_KW].astype(gq_ref.dtype)
    gk_ref[...] = proj[:, o + GLA_KW:o + 2 * GLA_KW].astype(gk_ref.dtype)
    gv_ref[...] = proj[:, o + 2 * GLA_KW:o + 2 * GLA_KW + GLA_VW].astype(gv_ref.dtype)
    gg_ref[...] = proj[:, o + 2 * GLA_KW + GLA_VW:o + 2 * GLA_KW + 2 * GLA_VW].astype(gg_ref.dtype)
    lora = proj[:, o + 2 * GLA_KW + 2 * GLA_VW:]
    la_lin = _dot(lora, wa2_ref[...]) + ba_ref[...]
    la_ref[...] = -_softplus(-la_lin) / GLA_GATE_NORMALIZER


def _pad_cols(w, n):
    return jnp.pad(w, ((0, 0), (0, n - w.shape[1])))


def _pad_rows(w, n):
    return jnp.pad(w, ((0, n - w.shape[0]), (0, 0)))


def _even_in(x2, seq_len, gn, w_in, mu_rkv, mu_wag, w0, w1, w2, a0, a1, a2, g1, g2,
             wa1, wa2, ba):
    n, d = x2.shape
    tm = TM_PROJ
    win = jnp.concatenate([w_in, _pad_cols(wa1, LANES)], axis=1).astype(BF16)
    w1p, w2p = _pad_cols(w1, LANES).astype(BF16), _pad_rows(w2, LANES).astype(BF16)
    a1p, a2p = _pad_cols(a1, LANES).astype(BF16), _pad_rows(a2, LANES).astype(BF16)
    g1p, g2p = _pad_cols(g1, 2 * LANES).astype(BF16), _pad_rows(g2, 2 * LANES).astype(BF16)
    wa2p = _pad_rows(wa2, LANES).astype(BF16)
    consts = [gn.reshape(1, d), win, mu_wag, mu_rkv.reshape(1, -1),
              w1p, w2p, w0.reshape(1, -1), a1p, a2p, a0.reshape(1, -1),
              g1p, g2p, wa2p, ba.reshape(1, -1)]
    widths = [RW_WIDTH] * 6 + [GLA_KW, GLA_KW, GLA_VW, GLA_VW, GLA_KW]
    dtypes = [BF16, BF16, BF16, F32, F32, BF16, BF16, BF16, BF16, BF16, F32]
    row = lambda w: pl.BlockSpec((tm, w), lambda i: (i, 0))
    return pl.pallas_call(
        functools.partial(_even_in_kernel, tiles_per_seq=seq_len // tm),
        grid=(n // tm,),
        in_specs=[row(d)] + [_const_spec(c.shape) for c in consts],
        out_specs=[row(w) for w in widths],
        out_shape=[jax.ShapeDtypeStruct((n, w), dt) for w, dt in zip(widths, dtypes)],
        scratch_shapes=[pltpu.VMEM((8, d), F32), pltpu.VMEM((8, 3 * RW_WIDTH), F32)],
        compiler_params=_params("arbitrary"),
        name="even_in",
    )(x2, *consts)


def _lane_half_masks(shape):
    lane = lax.broadcasted_iota(jnp.int32, shape, len(shape) - 1)
    return lane < 64, lane >= 64


def _stack_heads(x):
    m0, m1 = _lane_half_masks(x.shape)
    return jnp.concatenate([jnp.where(m0, x, 0.0), jnp.where(m1, x, 0.0)], axis=0)


def _tri_masks(n, blk):
    row = lax.broadcasted_iota(jnp.int32, (n, n), 0)
    col = lax.broadcasted_iota(jnp.int32, (n, n), 1)
    shift = int(math.log2(blk))
    same = jnp.right_shift(row, shift) == jnp.right_shift(col, shift)
    return same & (col < row), same & (col <= row)


def _rwkv_kernel(r_ref, k_ref, v_ref, lw_ref, eta_ref, g_ref,
                 kk_ref, ka_ref, rk_ref, lnw_ref, lnb_ref, o_ref,
                 h_ref, rg_ref, hadd_ref, y_ref, gam_ref):
    t = pl.program_id(1)

    @pl.when(t == 0)
    def _():
        h_ref[...] = jnp.zeros_like(h_ref)

    c = CHUNK
    c2 = 2 * c
    n_pairs = RW_WIDTH // LANES
    n_chunks = r_ref.shape[0] // c
    strict, incl = _tri_masks(c2, c)
    row = lax.broadcasted_iota(jnp.int32, (c, c), 0)
    col = lax.broadcasted_iota(jnp.int32, (c, c), 1)
    cum_mat = (col <= row).astype(BF16)
    ri = lax.broadcasted_iota(jnp.int32, (LANES, LANES), 0)
    ci_ = lax.broadcasted_iota(jnp.int32, (LANES, LANES), 1)
    head_ones = (jnp.right_shift(ri, 6) == jnp.right_shift(ci_, 6)).astype(BF16)
    eye = (ri == ci_).astype(F32)

    def chunk_rows(ci):
        return pl.ds(pl.multiple_of(ci * c, c), c)

    def lanes_of(p):
        return slice(p * LANES, (p + 1) * LANES)

    def k_terms(rows, p):
        k = k_ref[rows, lanes_of(p)]
        eta = eta_ref[rows, lanes_of(p)]
        return k, eta, k * (1.0 + (eta - 1.0) * ka_ref[:, lanes_of(p)])

    def prep(grp, carry):
        probs = [(grp * RW_PREP + cc, p) for cc in range(RW_PREP) for p in range(n_pairs)]
        each = lambda f: [f(ci, chunk_rows(ci), p) for ci, p in probs]
        zipped = lambda f, *ls: [f(*xs) for xs in zip(*ls)]

        kt = each(lambda ci, rows, p: k_terms(rows, p))
        k2 = [x[2] for x in kt]
        eta = [x[1] for x in kt]
        kk0 = each(lambda ci, rows, p: k_ref[rows, lanes_of(p)] * kk_ref[:, lanes_of(p)])
        ss = [_exact_dot_r(x * x, head_ones) for x in kk0]
        kk = zipped(lambda x, s: x / jnp.maximum(jnp.sqrt(s), 1e-12), kk0, ss)
        lw = each(lambda ci, rows, p: lw_ref[rows, lanes_of(p)])
        cum = [_exact_dot(cum_mat, x) for x in lw]
        cum_end = [x[c - 1:c, :] for x in cum]
        e_neg = [jnp.exp(-x) for x in cum]
        e_rel = zipped(lambda x, e: jnp.exp(e - x), cum, cum_end)
        b_vec = zipped(lambda x, e: x * e, kk, eta)
        a_st = zipped(lambda x, cm, l: _stack_heads(-x * jnp.exp(cm - l)), kk, cum, lw)
        r_st = each(lambda ci, rows, p: r_ref[rows, lanes_of(p)])
        r_st = zipped(lambda x, cm: _stack_heads(x * jnp.exp(cm)), r_st, cum)
        k_st = zipped(lambda x, e: _stack_heads(x * e), k2, e_neg)
        b_st = zipped(lambda x, e: _stack_heads(x * e), b_vec, e_neg)
        kh_t = zipped(lambda x, e: _stack_heads(x * e).T, k2, e_rel)
        bh_t = zipped(lambda x, e: _stack_heads(x * e).T, b_vec, e_rel)
        v_st = each(lambda ci, rows, p: _stack_heads(v_ref[rows, lanes_of(p)]))

        big = zipped(lambda a, r, b, k: _dot_nt(jnp.concatenate([a, r], axis=0),
                                                jnp.concatenate([b, k], axis=0)),
                     a_st, r_st, b_st, k_st)
        a_ab = [jnp.where(strict, x[:c2, :c2], 0.0) for x in big]
        a_ak = [jnp.where(strict, x[:c2, c2:], 0.0) for x in big]
        a_rb = [jnp.where(incl, x[c2:, :c2], 0.0) for x in big]
        a_rk = [jnp.where(incl, x[c2:, c2:], 0.0) for x in big]

        x2 = zipped(_dot, a_ak, v_st)
        m = zipped(_dot, a_ab, a_ab)
        tinv = [eye + x for x in a_ab]
        for _ in range(int(math.log2(c)) - 2):
            prod = zipped(lambda mm, pp: _dot(mm, jnp.concatenate([pp, mm], axis=1)), m, tinv)
            tinv = zipped(lambda pp, pr: pp + pr[:, :c2], tinv, prod)
            m = [pr[:, c2:] for pr in prod]
        tinv = zipped(lambda pp, mm: pp + _dot(mm, pp), tinv, m)

        wu = zipped(lambda ti, a, x: _dot(ti, jnp.concatenate([a, x], axis=1)), tinv, a_st, x2)
        w = [x[:, :LANES] for x in wu]
        vu = zipped(lambda vv, x: jnp.concatenate([vv, x[:, LANES:]], axis=0), v_st, wu)
        y0 = zipped(lambda rk, rb, x: _dot(jnp.concatenate([rk, rb], axis=1), x), a_rk, a_rb, vu)
        hadd = zipped(lambda kt_, bt_, x: _dot(jnp.concatenate([kt_, bt_], axis=1), x), kh_t, bh_t, vu)
        gr = zipped(lambda rb, bt_, ww: _dot(jnp.concatenate([rb, bt_], axis=0), ww), a_rb, bh_t, w)
        for i, (ci, p) in enumerate(probs):
            rg_ref[ci, p, :c2] = (r_st[i] + gr[i][:c2]).astype(rg_ref.dtype)
            rg_ref[ci, p, c2:] = gr[i][c2:].astype(rg_ref.dtype)
            hadd_ref[ci, p] = hadd[i]
            y_ref[ci, p] = y0[i]
            gam_ref[ci, p] = jnp.exp(jnp.broadcast_to(cum_end[i], (LANES, LANES))).T
        return carry

    lax.fori_loop(0, n_chunks // RW_PREP, prep, 0)

    def advance(ci, carry):
        h0 = [h_ref[p] for p in range(n_pairs)]
        prod = [_dot(rg_ref[ci, p], h0[p]) for p in range(n_pairs)]
        for p in range(n_pairs):
            y_ref[ci, p] = y_ref[ci, p] + prod[p][:c2]
            h_ref[p] = gam_ref[ci, p] * h0[p] + prod[p][c2:] + hadd_ref[ci, p]
        return carry

    lax.fori_loop(0, n_chunks, advance, 0)

    def finish(ci, carry):
        rows = chunk_rows(ci)
        ys = [y_ref[ci, p] for p in range(n_pairs)]
        y = [x[:c] + x[c:] for x in ys]
        mean = [_exact_dot_r(x, head_ones) / RW_HEAD_DIM for x in y]
        yc = [x - mu for x, mu in zip(y, mean)]
        var = [_exact_dot_r(x * x, head_ones) / RW_HEAD_DIM for x in yc]
        rk2 = [r_ref[rows, lanes_of(p)] * k_terms(rows, p)[2] * rk_ref[:, lanes_of(p)]
               for p in range(n_pairs)]
        bsum = [_exact_dot_r(x, head_ones) for x in rk2]
        for p in range(n_pairs):
            lanes = lanes_of(p)
            yn = yc[p] * lax.rsqrt(var[p] + RW_LN_EPS) * lnw_ref[:, lanes] + lnb_ref[:, lanes]
            out = (yn + bsum[p] * v_ref[rows, lanes]) * g_ref[rows, lanes]
            o_ref[rows, lanes] = out.astype(o_ref.dtype)
        return carry

    lax.fori_loop(0, n_chunks, finish, 0)


def _rwkv(batch, seq_len, r, k, v, lw, eta, g, k_k, k_a, r_k, ln_w, ln_b):
    n = r.shape[0]
    tb = TB_REC
    nt = seq_len // tb
    n_pairs = RW_WIDTH // LANES
    per_chunk = (tb // CHUNK, n_pairs)
    row = pl.BlockSpec((tb, RW_WIDTH), lambda b, t: (b * nt + t, 0))
    vecs = [p.reshape(1, RW_WIDTH) for p in (k_k, k_a, r_k, ln_w, ln_b)]
    return pl.pallas_call(
        _rwkv_kernel,
        grid=(batch, nt),
        in_specs=[row] * 6 + [_const_spec((1, RW_WIDTH))] * 5,
        out_specs=row,
        out_shape=jax.ShapeDtypeStruct((n, RW_WIDTH), BF16),
        scratch_shapes=[pltpu.VMEM((n_pairs, LANES, LANES), F32),
                        pltpu.VMEM(per_chunk + (2 * LANES, LANES), BF16),
                        pltpu.VMEM(per_chunk + (LANES, LANES), F32),
                        pltpu.VMEM(per_chunk + (LANES, LANES), F32),
                        pltpu.VMEM(per_chunk + (LANES, LANES), F32)],
        compiler_params=_params("arbitrary", "arbitrary"),
        name="rwkv7",
    )(r, k, v, lw, eta, g, *vecs)


def _gla_kernel(q_ref, k_ref, v_ref, gg_ref, la_ref, nrm_ref, o_ref,
                s_ref, qst_ref, intra_ref, kv_ref, gam_ref):
    t = pl.program_id(1)

    @pl.when(t == 0)
    def _():
        s_ref[...] = jnp.zeros_like(s_ref)

    c = CHUNK
    n_pairs = GLA_KW // LANES
    n_chunks = q_ref.shape[0] // c
    row = lax.broadcasted_iota(jnp.int32, (c, c), 0)
    col = lax.broadcasted_iota(jnp.int32, (c, c), 1)
    cum_mat = (col <= row).astype(BF16)
    _, causal = _tri_masks(2 * c, c)
    scale = GLA_DK ** -0.5

    probs = [(ci, p) for ci in range(n_chunks) for p in range(n_pairs)]
    rows_of = lambda ci: slice(ci * c, (ci + 1) * c)
    lanes_of = lambda p: slice(p * LANES, (p + 1) * LANES)
    zipped = lambda f, *ls: [f(*xs) for xs in zip(*ls)]
    cum = [_exact_dot(cum_mat, la_ref[rows_of(ci), lanes_of(p)]) for ci, p in probs]
    cum_end = [x[c - 1:c, :] for x in cum]
    q_st = [_stack_heads(q_ref[rows_of(ci), lanes_of(p)] * scale * jnp.exp(x))
            for (ci, p), x in zip(probs, cum)]
    k_st = [_stack_heads(k_ref[rows_of(ci), lanes_of(p)] * jnp.exp(-x))
            for (ci, p), x in zip(probs, cum)]
    kh_t = [_stack_heads(k_ref[rows_of(ci), lanes_of(p)] * jnp.exp(e - x)).T
            for (ci, p), x, e in zip(probs, cum, cum_end)]
    v_st = [jnp.concatenate([v_ref[rows_of(ci), (2 * p) * GLA_DV:(2 * p + 1) * GLA_DV],
                             v_ref[rows_of(ci), (2 * p + 1) * GLA_DV:(2 * p + 2) * GLA_DV]], axis=0)
            for ci, p in probs]
    scores = zipped(lambda a, b: jnp.where(causal, _dot_nt(a, b), 0.0), q_st, k_st)
    intra = zipped(_dot, scores, v_st)
    kv = zipped(_dot, kh_t, v_st)
    for i, (ci, p) in enumerate(probs):
        qst_ref[ci, p] = q_st[i].astype(qst_ref.dtype)
        intra_ref[ci, p] = intra[i]
        kv_ref[ci, p] = kv[i]
        gam_ref[ci, p] = jnp.exp(jnp.broadcast_to(cum_end[i], (LANES, LANES))).T

    def advance(ci, carry):
        rows = pl.ds(pl.multiple_of(ci * c, c), c)
        s0 = [s_ref[p] for p in range(n_pairs)]
        o_st = [intra_ref[ci, p] + _dot(qst_ref[ci, p], s0[p]) for p in range(n_pairs)]
        for p in range(n_pairs):
            s_ref[p] = gam_ref[ci, p] * s0[p] + kv_ref[ci, p]
            for hh in range(2):
                vl = slice((2 * p + hh) * GLA_DV, (2 * p + hh + 1) * GLA_DV)
                o = o_st[p][hh * c:(hh + 1) * c]
                o = o * lax.rsqrt(jnp.mean(o * o, axis=-1, keepdims=True) + GLA_NORM_EPS)
                o = o * nrm_ref[:, vl]
                gg = gg_ref[rows, vl].astype(F32)
                o_ref[rows, vl] = (o * (gg * _sigmoid(gg))).astype(o_ref.dtype)
        return carry

    lax.fori_loop(0, n_chunks, advance, 0)


def _gla(batch, seq_len, gq, gk, gv, gg, la, gla_norm):
    n = gq.shape[0]
    tb = TB_REC
    nt = seq_len // tb
    n_pairs = GLA_KW // LANES
    per_chunk = (tb // CHUNK, n_pairs)
    rowk = pl.BlockSpec((tb, GLA_KW), lambda b, t: (b * nt + t, 0))
    rowv = pl.BlockSpec((tb, GLA_VW), lambda b, t: (b * nt + t, 0))
    return pl.pallas_call(
        _gla_kernel,
        grid=(batch, nt),
        in_specs=[rowk, rowk, rowv, rowv, rowk, _const_spec((1, GLA_VW))],
        out_specs=rowv,
        out_shape=jax.ShapeDtypeStruct((n, GLA_VW), BF16),
        scratch_shapes=[pltpu.VMEM((n_pairs, LANES, GLA_DV), F32),
                        pltpu.VMEM(per_chunk + (LANES, LANES), BF16),
                        pltpu.VMEM(per_chunk + (LANES, GLA_DV), F32),
                        pltpu.VMEM(per_chunk + (LANES, GLA_DV), F32),
                        pltpu.VMEM(per_chunk + (LANES, GLA_DV), F32)],
        compiler_params=_params("arbitrary", "arbitrary"),
        name="gla",
    )(gq, gk, gv, gg, la, gla_norm.reshape(1, GLA_VW))


def _odd_in_kernel(x_ref, gn_ref, w_ref, q_ref, k_ref, vt_ref):
    h = _rms_norm(x_ref[...], gn_ref[...], NORM_EPS)
    qkv = _dot(h, w_ref[...])
    d = q_ref.shape[1]
    q_ref[...] = (qkv[:, :d] * (DA_QK_DIM ** -0.5 * LOG2E)).astype(q_ref.dtype)
    k_ref[...] = qkv[:, d:2 * d].astype(k_ref.dtype)
    tk = vt_ref.shape[3]
    for sub in range(vt_ref.shape[1]):
        vt_ref[0, sub] = qkv[sub * tk:(sub + 1) * tk, 2 * d:].T.astype(vt_ref.dtype)


def _odd_in(x2, seq_len, gn, w_qkv):
    n, d = x2.shape
    tm, tk = TM_PROJ, T_ATT
    nt, sub = seq_len // tm, tm // tk
    row = pl.BlockSpec((tm, d), lambda i: (i, 0))
    return pl.pallas_call(
        _odd_in_kernel,
        grid=(n // tm,),
        in_specs=[row, _const_spec((1, d)), _const_spec(w_qkv.shape)],
        out_specs=[row, row, pl.BlockSpec((1, sub, d, tk), lambda i: (i // nt, i % nt, 0, 0))],
        out_shape=[jax.ShapeDtypeStruct((n, d), BF16)] * 2
        + [jax.ShapeDtypeStruct((n // seq_len, seq_len // tk, d, tk), BF16)],
        compiler_params=_params("arbitrary"),
        name="odd_in",
    )(x2, gn.reshape(1, d), w_qkv.astype(BF16))


def _attn_kernel(lam_ref, q_ref, k_ref, vt_ref, bias_ref, subln_ref, o_ref,
                 m_ref, acc_ref, qst_ref, p_ref, alpha_ref, *, lam_init):
    i = pl.program_id(2)
    tq = q_ref.shape[0]
    tk = vt_ref.shape[3]
    ratio = tq // tk
    n_blocks = ratio * (i + 1)
    dv = vt_ref.shape[2]

    qst_ref[...] = _stack_heads(q_ref[...].astype(F32)).T.astype(BF16)
    m_ref[...] = jnp.full_like(m_ref, -jnp.inf)
    acc_ref[...] = jnp.zeros_like(acc_ref)
    ones_rows = jnp.ones((acc_ref.shape[0] - dv, tk), BF16)

    def accumulate(j, slot):
        lhs = jnp.concatenate([vt_ref[0, j], ones_rows], axis=0)
        acc_ref[...] = alpha_ref[slot] * acc_ref[...] + jnp.dot(
            lhs, p_ref[slot], preferred_element_type=F32)

    p_ref[1] = jnp.zeros(p_ref.shape[1:], p_ref.dtype)
    alpha_ref[1] = jnp.ones(alpha_ref.shape[1:], F32)

    def stage(j, cur, prev):
        k_blk = k_ref[pl.ds(pl.multiple_of(j * tk, tk), tk), :]
        tile = jnp.minimum(n_blocks - 1 - j, ratio + 1)
        for g in range(2 * tq // ATT_GROUP):
            cols = slice(g * ATT_GROUP, (g + 1) * ATT_GROUP)
            s = jnp.dot(k_blk, qst_ref[:, cols], preferred_element_type=F32)
            s = s + bias_ref[0, tile, :, cols]
            m_prev = m_ref[:, cols]
            m_new = jnp.maximum(m_prev, jnp.max(s, axis=0, keepdims=True))
            p_ref[cur, :, cols] = jnp.exp2(s - m_new).astype(p_ref.dtype)
            alpha_ref[cur, :, cols] = jnp.exp2(m_prev - m_new)
            m_ref[:, cols] = m_new
        accumulate(jnp.maximum(j - 1, 0), prev)

    def body(jj, carry):
        stage(2 * jj, 0, 1)
        stage(2 * jj + 1, 1, 0)
        return carry

    lax.fori_loop(0, n_blocks // 2, body, 0)
    accumulate(n_blocks - 1, 1)

    lam_v = lam_ref[...]
    lam = (jnp.exp(jnp.sum(lam_v[0:1] * lam_v[1:2], axis=1, keepdims=True))
           - jnp.exp(jnp.sum(lam_v[2:3] * lam_v[3:4], axis=1, keepdims=True)) + lam_init)
    o_all = acc_ref[:dv, :] * (1.0 / acc_ref[dv:dv + 1, :])
    o = o_all[:, :tq] - lam * o_all[:, tq:]
    o = o * lax.rsqrt(jnp.mean(o * o, axis=0, keepdims=True) + DA_SUBLN_EPS)
    o = o * subln_ref[...] * (1.0 - lam_init)
    o_ref[...] = o.T.astype(o_ref.dtype)


def _t5_causal_buckets(dist):
    max_exact = REL_BUCKETS // 2
    ratio = jnp.maximum(dist, max_exact).astype(F32) / max_exact
    large = max_exact + (jnp.log(ratio) / math.log(REL_MAX_DIST / max_exact)
                         * (REL_BUCKETS - max_exact)).astype(jnp.int32)
    large = jnp.minimum(large, REL_BUCKETS - 1)
    return jnp.where(dist < max_exact, dist, large)


def _attn_bias_tiles(rel_bias, tq, tk):
    heads = rel_bias.shape[1]
    ratio = tq // tk
    span = tq + tk
    by_dist = rel_bias[_t5_causal_buckets(jnp.arange(span + tk, dtype=jnp.int32))].T
    idx = np.arange(span)
    delta = np.where(idx < tq, idx, idx - span)

    def toeplitz(vec):
        flat = jnp.tile(vec, (1, tk))[:, :tk * (span - 1)]
        return flat.reshape(heads, tk, span - 1)[:, :, :tq]

    tiles = []
    for d in range(ratio + 1):
        dist = delta + tk * (d - ratio + 1)
        vals = by_dist[:, np.clip(dist, 0, None)]
        tiles.append(toeplitz(jnp.where(dist >= 0, vals, NEG_INF)))
    assert tk * 2 - (tk - 1) > REL_MAX_DIST
    tiles.append(jnp.broadcast_to(rel_bias[REL_BUCKETS - 1][:, None, None], tiles[0].shape))
    tiles = jnp.stack(tiles, axis=1)
    return (jnp.concatenate([tiles, tiles], axis=3) * LOG2E).astype(F32)


def _diff_attn(batch, seq_len, q, k, vt, lam_params, subln, bias_tiles, lam_init):
    n, d = q.shape
    heads = d // DA_V_DIM
    tq, tk = TQ_ATT, T_ATT
    nq = seq_len // tq
    assert (tq // tk) % 2 == 0
    qspec = pl.BlockSpec((tq, DA_V_DIM), lambda h, b, i: (b * nq + i, h))
    subln_cols = jnp.broadcast_to(subln.reshape(DA_V_DIM, 1), (DA_V_DIM, tq))
    return pl.pallas_call(
        functools.partial(_attn_kernel, lam_init=lam_init),
        grid=(heads, batch, nq),
        in_specs=[_const_spec(lam_params.shape), qspec,
                  pl.BlockSpec((seq_len, DA_V_DIM), lambda h, b, i: (b, h)),
                  pl.BlockSpec((1, seq_len // tk, DA_V_DIM, tk), lambda h, b, i: (b, 0, h, 0)),
                  pl.BlockSpec((1,) + bias_tiles.shape[1:], lambda h, b, i: (h, 0, 0, 0)),
                  _const_spec((DA_V_DIM, tq))],
        out_specs=qspec,
        out_shape=jax.ShapeDtypeStruct((n, d), BF16),
        scratch_shapes=[pltpu.VMEM((1, 2 * tq), F32),
                        pltpu.VMEM((DA_V_DIM + BF16_ROWS, 2 * tq), F32),
                        pltpu.VMEM((DA_V_DIM, 2 * tq), BF16), pltpu.VMEM((2, tk, 2 * tq), BF16),
                        pltpu.VMEM((2, 1, 2 * tq), F32)],
        compiler_params=_params("arbitrary", "arbitrary", "arbitrary"),
        name="diff_attn",
    )(lam_params, q, k, vt, bias_tiles, subln_cols)


def _ffn_kernel(*refs, n_mix, final):
    x_ref = refs[0]
    mix_refs = refs[1:1 + n_mix]
    wout_ref, gn_ref, wg_ref, wu_ref, wd_ref = refs[1 + n_mix:6 + n_mix]
    gfin_ref = refs[6 + n_mix] if final else None
    o_ref = refs[-1]

    mix = jnp.concatenate([m_ref[...] for m_ref in mix_refs], axis=1)
    x = x_ref[...] + jnp.dot(mix, wout_ref[...], preferred_element_type=F32)
    h = _rms_norm(x, gn_ref[...], NORM_EPS).astype(BF16)
    acc = x
    hidden = wg_ref.shape[1]
    for c0 in range(0, hidden, FFN_HC):
        cols = slice(c0, c0 + FFN_HC)
        gate = jnp.dot(h, wg_ref[:, cols], preferred_element_type=F32)
        up = jnp.dot(h, wu_ref[:, cols], preferred_element_type=F32)
        act = (gate * _sigmoid(gate)) * up
        acc = acc + jnp.dot(act.astype(BF16), wd_ref[cols, :], preferred_element_type=F32)
    if final:
        acc = _rms_norm(acc, gfin_ref[...], NORM_EPS)
    o_ref[...] = acc


def _ffn(x2, mixes, w_out, gn, wg, wu, wd, g_final=None):
    n, d = x2.shape
    tm = TM_FFN
    final = g_final is not None
    row = lambda w: pl.BlockSpec((tm, w), lambda i: (i, 0))
    consts = [w_out.astype(BF16), gn.reshape(1, d), wg.astype(BF16), wu.astype(BF16),
              wd.astype(BF16)]
    if final:
        consts.append(g_final.reshape(1, d))
    return pl.pallas_call(
        functools.partial(_ffn_kernel, n_mix=len(mixes), final=final),
        grid=(n // tm,),
        in_specs=([row(d)] + [row(m.shape[1]) for m in mixes]
                  + [_const_spec(c.shape) for c in consts]),
        out_specs=row(d),
        out_shape=jax.ShapeDtypeStruct((n, d), F32),
        compiler_params=_params("arbitrary"),
        name="ffn",
    )(x2, *mixes, *consts)


def _diff_lambda_init(layer):
    return 0.8 - 0.6 * math.exp(-0.3 * layer)


def kernel(x, rel_bias, norm_mix, norm_ffn, norm_final, ab_w_in, ab_w_out, rw_mu_rkv, rw_mu_wag, rw_w0, rw_w1, rw_w2, rw_a0, rw_a1, rw_a2, rw_g1, rw_g2, rw_k_k, rw_k_a, rw_r_k, rw_ln_w, rw_ln_b, gla_wa1, gla_wa2, gla_ba, gla_norm, da_w_qkv, da_w_out, da_lam_q1, da_lam_k1, da_lam_q2, da_lam_k2, da_subln, ffn_w_gate, ffn_w_up, ffn_w_down):
    batch, seq_len, d = x.shape
    depth = norm_mix.shape[0]
    x2 = x.reshape(batch * seq_len, d)
    bias_tiles = _attn_bias_tiles(rel_bias, TQ_ATT, T_ATT)
    for layer in range(depth):
        i = layer // 2
        if layer % 2 == 0:
            (r, k, v, lw, eta, g, gq, gk, gv, gg, la) = _even_in(
                x2, seq_len, norm_mix[layer], ab_w_in[i], rw_mu_rkv[i], rw_mu_wag[i],
                rw_w0[i], rw_w1[i], rw_w2[i], rw_a0[i], rw_a1[i], rw_a2[i], rw_g1[i], rw_g2[i],
                gla_wa1[i], gla_wa2[i], gla_ba[i])
            o_a = _rwkv(batch, seq_len, r, k, v, lw, eta, g, rw_k_k[i], rw_k_a[i],
                        rw_r_k[i], rw_ln_w[i], rw_ln_b[i])
            o_b = _gla(batch, seq_len, gq, gk, gv, gg, la, gla_norm[i])
            mixes = [o_a, o_b]
            w_out = ab_w_out[i]
        else:
            q, k, vt = _odd_in(x2, seq_len, norm_mix[layer], da_w_qkv[i])
            lam_params = jnp.stack([da_lam_q1[i], da_lam_k1[i], da_lam_q2[i], da_lam_k2[i]])
            o = _diff_attn(batch, seq_len, q, k, vt, lam_params, da_subln[i], bias_tiles,
                           _diff_lambda_init(layer))
            mixes = [o]
            w_out = da_w_out[i]
        x2 = _ffn(x2, mixes, w_out, norm_ffn[layer], ffn_w_gate[layer], ffn_w_up[layer],
                  ffn_w_down[layer], norm_final if layer == depth - 1 else None)
    return x2.reshape(batch, seq_len, d)
```

```python
import functools
import math

import jax
import jax.numpy as jnp
import numpy as np
from jax import lax
from jax.experimental import pallas as pl
from jax.experimental.pallas import tpu as pltpu

F32 = jnp.float32
BF16 = jnp.bfloat16

RW_HEAD_DIM = 64
RW_WIDTH = 512
RW_LN_EPS = 64e-5
GLA_HEADS = 4
GLA_DK = 64
GLA_DV = 128
GLA_KW = 256
GLA_VW = 512
GLA_GATE_NORMALIZER = 16.0
GLA_NORM_EPS = 1e-5
DA_QK_DIM = 64
DA_V_DIM = 128
DA_SUBLN_EPS = 1e-5
NEG_INF = -1e30
REL_BUCKETS = 32
REL_MAX_DIST = 128
NORM_EPS = 1e-6
LOG2E = math.log2(math.e)

LANES = 128
BF16_ROWS = 16
CHUNK = 64
VMEM_LIMIT = 56 * 1024 * 1024

TM_PROJ = 512
TM_FFN = 1024
FFN_HC = 256
TB_REC = 512
RW_PREP = 4
T_ATT = 256
TQ_ATT = 512
ATT_GROUP = 128


def _dot(a, b):
    return jnp.dot(a.astype(BF16), b.astype(BF16), preferred_element_type=F32)


def _dot_nt(a, b):
    return lax.dot_general(a.astype(BF16), b.astype(BF16),
                           (((1,), (1,)), ((), ())), preferred_element_type=F32)


def _split3(x):
    hi = x.astype(BF16)
    r1 = x - hi.astype(F32)
    mid = r1.astype(BF16)
    lo = (r1 - mid.astype(F32)).astype(BF16)
    return hi, mid, lo


def _exact_dot(m_bf16, x):
    w = x.shape[-1]
    parts = jnp.concatenate(_split3(x), axis=-1)
    y = jnp.dot(m_bf16, parts, preferred_element_type=F32)
    return y[:, :w] + y[:, w:2 * w] + y[:, 2 * w:]


def _exact_dot_r(x, m_bf16):
    hi, mid, lo = _split3(x)
    parts = jnp.concatenate([hi, mid, lo], axis=0)
    y = jnp.dot(parts, m_bf16, preferred_element_type=F32)
    n = x.shape[0]
    return y[:n] + y[n:2 * n] + y[2 * n:]


def _rms_norm(x, g, eps):
    y = x * lax.rsqrt(jnp.mean(x * x, axis=-1, keepdims=True) + eps)
    return y * g


def _softplus(z):
    return jnp.maximum(z, 0.0) + jnp.log(1.0 + jnp.exp(-jnp.abs(z)))


def _sigmoid(z):
    return 1.0 / (1.0 + jnp.exp(-z))


def _const_spec(shape):
    nd = len(shape)
    return pl.BlockSpec(shape, lambda *_: (0,) * nd, pipeline_mode=pl.Buffered(1))


def _params(*sem):
    return pltpu.CompilerParams(dimension_semantics=sem, vmem_limit_bytes=VMEM_LIMIT)


def _shift_rows(cur, prev_row):
    rolled = pltpu.roll(cur, shift=1, axis=0)
    row = lax.broadcasted_iota(jnp.int32, cur.shape, 0)
    return jnp.where(row == 0, prev_row, rolled)


def _even_in_kernel(x_ref, gn_ref, win_ref, mu_wag_ref, mu_rkv_ref,
                    w1_ref, w2_ref, w0_ref, a1_ref, a2_ref, a0_ref,
                    g1_ref, g2_ref, wa2_ref, ba_ref,
                    r_ref, k_ref, v_ref, lw_ref, eta_ref, g_ref,
                    gq_ref, gk_ref, gv_ref, gg_ref, la_ref,
                    carry_h, carry_p, *, tiles_per_seq):
    tm = x_ref.shape[0]

    @pl.when(pl.program_id(0) % tiles_per_seq == 0)
    def _():
        carry_h[...] = jnp.zeros_like(carry_h)
        carry_p[...] = jnp.zeros_like(carry_p)

    h = _rms_norm(x_ref[...], gn_ref[...], NORM_EPS)
    h_prev = _shift_rows(h, carry_h[0:1, :])
    carry_h[0:1, :] = h[tm - 1:tm, :]
    dh = h_prev - h
    xw = h + dh * mu_wag_ref[0:1, :]
    xa = h + dh * mu_wag_ref[1:2, :]
    xg = h + dh * mu_wag_ref[2:3, :]

    proj = _dot(h, win_ref[...])
    p_rkv = proj[:, :3 * RW_WIDTH]
    p_prev = _shift_rows(p_rkv, carry_p[0:1, :])
    carry_p[0:1, :] = p_rkv[tm - 1:tm, :]
    rkv = p_rkv + (p_prev - p_rkv) * mu_rkv_ref[...]
    r_ref[...] = rkv[:, :RW_WIDTH].astype(r_ref.dtype)
    k_ref[...] = rkv[:, RW_WIDTH:2 * RW_WIDTH].astype(k_ref.dtype)
    v_ref[...] = rkv[:, 2 * RW_WIDTH:].astype(v_ref.dtype)

    w_lin = w0_ref[...] + _dot(jnp.tanh(_dot(xw, w1_ref[...])), w2_ref[...])
    w_raw = -_softplus(-w_lin) - 0.5
    lw_ref[...] = -jnp.exp(w_raw)
    eta_ref[...] = _sigmoid(a0_ref[...] + _dot(_dot(xa, a1_ref[...]), a2_ref[...]))
    g_ref[...] = _dot(_sigmoid(_dot(xg, g1_ref[...])), g2_ref[...]).astype(g_ref.dtype)

    o = 3 * RW_WIDTH
    gq_ref[...] = proj[:, o:o + GLA_KW].astype(gq_ref.dtype)
    gk_ref[...] = proj[:, o + GLA_KW:o + 2 * GLA_KW].astype(gk_ref.dtype)
    gv_ref[...] = proj[:, o + 2 * GLA_KW:o + 2 * GLA_KW + GLA_VW].astype(gv_ref.dtype)
    gg_ref[...] = proj[:, o + 2 * GLA_KW + GLA_VW:o + 2 * GLA_KW + 2 * GLA_VW].astype(gg_ref.dtype)
    lora = proj[:, o + 2 * GLA_KW + 2 * GLA_VW:]
    la_lin = _dot(lora, wa2_ref[...]) + ba_ref[...]
    la_ref[...] = -_softplus(-la_lin) / GLA_GATE_NORMALIZER


def _pad_cols(w, n):
    return jnp.pad(w, ((0, 0), (0, n - w.shape[1])))


def _pad_rows(w, n):
    return jnp.pad(w, ((0, n - w.shape[0]), (0, 0)))


def _even_in(x2, seq_len, gn, w_in, mu_rkv, mu_wag, w0, w1, w2, a0, a1, a2, g1, g2,
             wa1, wa2, ba):
    n, d = x2.shape
    tm = TM_PROJ
    win = jnp.concatenate([w_in, _pad_cols(wa1, LANES)], axis=1).astype(BF16)
    w1p, w2p = _pad_cols(w1, LANES).astype(BF16), _pad_rows(w2, LANES).astype(BF16)
    a1p, a2p = _pad_cols(a1, LANES).astype(BF16), _pad_rows(a2, LANES).astype(BF16)
    g1p, g2p = _pad_cols(g1, 2 * LANES).astype(BF16), _pad_rows(g2, 2 * LANES).astype(BF16)
    wa2p = _pad_rows(wa2, LANES).astype(BF16)
    consts = [gn.reshape(1, d), win, mu_wag, mu_rkv.reshape(1, -1),
              w1p, w2p, w0.reshape(1, -1), a1p, a2p, a0.reshape(1, -1),
              g1p, g2p, wa2p, ba.reshape(1, -1)]
    widths = [RW_WIDTH] * 6 + [GLA_KW, GLA_KW, GLA_VW, GLA_VW, GLA_KW]
    dtypes = [BF16, BF16, BF16, F32, F32, BF16, BF16, BF16, BF16, BF16, F32]
    row = lambda w: pl.BlockSpec((tm, w), lambda i: (i, 0))
    return pl.pallas_call(
        functools.partial(_even_in_kernel, tiles_per_seq=seq_len // tm),
        grid=(n // tm,),
        in_specs=[row(d)] + [_const_spec(c.shape) for c in consts],
        out_specs=[row(w) for w in widths],
        out_shape=[jax.ShapeDtypeStruct((n, w), dt) for w, dt in zip(widths, dtypes)],
        scratch_shapes=[pltpu.VMEM((8, d), F32), pltpu.VMEM((8, 3 * RW_WIDTH), F32)],
        compiler_params=_params("arbitrary"),
        name="even_in",
    )(x2, *consts)


def _lane_half_masks(shape):
    lane = lax.broadcasted_iota(jnp.int32, shape, len(shape) - 1)
    return lane < 64, lane >= 64


def _stack_heads(x):
    m0, m1 = _lane_half_masks(x.shape)
    return jnp.concatenate([jnp.where(m0, x, 0.0), jnp.where(m1, x, 0.0)], axis=0)


def _tri_masks(n, blk):
    row = lax.broadcasted_iota(jnp.int32, (n, n), 0)
    col = lax.broadcasted_iota(jnp.int32, (n, n), 1)
    shift = int(math.log2(blk))
    same = jnp.right_shift(row, shift) == jnp.right_shift(col, shift)
    return same & (col < row), same & (col <= row)


def _rwkv_kernel(r_ref, k_ref, v_ref, lw_ref, eta_ref, g_ref,
                 kk_ref, ka_ref, rk_ref, lnw_ref, lnb_ref, o_ref,
                 h_ref, rg_ref, hadd_ref, y_ref, gam_ref):
    t = pl.program_id(1)

    @pl.when(t == 0)
    def _():
        h_ref[...] = jnp.zeros_like(h_ref)

    c = CHUNK
    c2 = 2 * c
    n_pairs = RW_WIDTH // LANES
    n_chunks = r_ref.shape[0] // c
    strict, incl = _tri_masks(c2, c)
    row = lax.broadcasted_iota(jnp.int32, (c, c), 0)
    col = lax.broadcasted_iota(jnp.int32, (c, c), 1)
    cum_mat = (col <= row).astype(BF16)
    ri = lax.broadcasted_iota(jnp.int32, (LANES, LANES), 0)
    ci_ = lax.broadcasted_iota(jnp.int32, (LANES, LANES), 1)
    head_ones = (jnp.right_shift(ri, 6) == jnp.right_shift(ci_, 6)).astype(BF16)
    eye = (ri == ci_).astype(F32)

    def chunk_rows(ci):
        return pl.ds(pl.multiple_of(ci * c, c), c)

    def lanes_of(p):
        return slice(p * LANES, (p + 1) * LANES)

    def k_terms(rows, p):
        k = k_ref[rows, lanes_of(p)]
        eta = eta_ref[rows, lanes_of(p)]
        return k, eta, k * (1.0 + (eta - 1.0) * ka_ref[:, lanes_of(p)])

    def prep(grp, carry):
        probs = [(grp * RW_PREP + cc, p) for cc in range(RW_PREP) for p in range(n_pairs)]
        each = lambda f: [f(ci, chunk_rows(ci), p) for ci, p in probs]
        zipped = lambda f, *ls: [f(*xs) for xs in zip(*ls)]

        kt = each(lambda ci, rows, p: k_terms(rows, p))
        k2 = [x[2] for x in kt]
        eta = [x[1] for x in kt]
        kk0 = each(lambda ci, rows, p: k_ref[rows, lanes_of(p)] * kk_ref[:, lanes_of(p)])
        ss = [_exact_dot_r(x * x, head_ones) for x in kk0]
        kk = zipped(lambda x, s: x / jnp.maximum(jnp.sqrt(s), 1e-12), kk0, ss)
        lw = each(lambda ci, rows, p: lw_ref[rows, lanes_of(p)])
        cum = [_exact_dot(cum_mat, x) for x in lw]
        cum_end = [x[c - 1:c, :] for x in cum]
        e_neg = [jnp.exp(-x) for x in cum]
        e_rel = zipped(lambda x, e: jnp.exp(e - x), cum, cum_end)
        b_vec = zipped(lambda x, e: x * e, kk, eta)
        a_st = zipped(lambda x, cm, l: _stack_heads(-x * jnp.exp(cm - l)), kk, cum, lw)
        r_st = each(lambda ci, rows, p: r_ref[rows, lanes_of(p)])
        r_st = zipped(lambda x, cm: _stack_heads(x * jnp.exp(cm)), r_st, cum)
        k_st = zipped(lambda x, e: _stack_heads(x * e), k2, e_neg)
        b_st = zipped(lambda x, e: _stack_heads(x * e), b_vec, e_neg)
        kh_t = zipped(lambda x, e: _stack_heads(x * e).T, k2, e_rel)
        bh_t = zipped(lambda x, e: _stack_heads(x * e).T, b_vec, e_rel)
        v_st = each(lambda ci, rows, p: _stack_heads(v_ref[rows, lanes_of(p)]))

        big = zipped(lambda a, r, b, k: _dot_nt(jnp.concatenate([a, r], axis=0),
                                                jnp.concatenate([b, k], axis=0)),
                     a_st, r_st, b_st, k_st)
        a_ab = [jnp.where(strict, x[:c2, :c2], 0.0) for x in big]
        a_ak = [jnp.where(strict, x[:c2, c2:], 0.0) for x in big]
        a_rb = [jnp.where(incl, x[c2:, :c2], 0.0) for x in big]
        a_rk = [jnp.where(incl, x[c2:, c2:], 0.0) for x in big]

        x2 = zipped(_dot, a_ak, v_st)
        m = zipped(_dot, a_ab, a_ab)
        tinv = [eye + x for x in a_ab]
        for _ in range(int(math.log2(c)) - 2):
            prod = zipped(lambda mm, pp: _dot(mm, jnp.concatenate([pp, mm], axis=1)), m, tinv)
            tinv = zipped(lambda pp, pr: pp + pr[:, :c2], tinv, prod)
            m = [pr[:, c2:] for pr in prod]
        tinv = zipped(lambda pp, mm: pp + _dot(mm, pp), tinv, m)

        wu = zipped(lambda ti, a, x: _dot(ti, jnp.concatenate([a, x], axis=1)), tinv, a_st, x2)
        w = [x[:, :LANES] for x in wu]
        vu = zipped(lambda vv, x: jnp.concatenate([vv, x[:, LANES:]], axis=0), v_st, wu)
        y0 = zipped(lambda rk, rb, x: _dot(jnp.concatenate([rk, rb], axis=1), x), a_rk, a_rb, vu)
        hadd = zipped(lambda kt_, bt_, x: _dot(jnp.concatenate([kt_, bt_], axis=1), x), kh_t, bh_t, vu)
        gr = zipped(lambda rb, bt_, ww: _dot(jnp.concatenate([rb, bt_], axis=0), ww), a_rb, bh_t, w)
        for i, (ci, p) in enumerate(probs):
            rg_ref[ci, p, :c2] = (r_st[i] + gr[i][:c2]).astype(rg_ref.dtype)
            rg_ref[ci, p, c2:] = gr[i][c2:].astype(rg_ref.dtype)
            hadd_ref[ci, p] = hadd[i]
            y_ref[ci, p] = y0[i]
            gam_ref[ci, p] = jnp.exp(jnp.broadcast_to(cum_end[i], (LANES, LANES))).T
        return carry

    lax.fori_loop(0, n_chunks // RW_PREP, prep, 0)

    def advance(ci, carry):
        h0 = [h_ref[p] for p in range(n_pairs)]
        prod = [_dot(rg_ref[ci, p], h0[p]) for p in range(n_pairs)]
        for p in range(n_pairs):
            y_ref[ci, p] = y_ref[ci, p] + prod[p][:c2]
            h_ref[p] = gam_ref[ci, p] * h0[p] + prod[p][c2:] + hadd_ref[ci, p]
        return carry

    lax.fori_loop(0, n_chunks, advance, 0)

    def finish(ci, carry):
        rows = chunk_rows(ci)
        ys = [y_ref[ci, p] for p in range(n_pairs)]
        y = [x[:c] + x[c:] for x in ys]
        mean = [_exact_dot_r(x, head_ones) / RW_HEAD_DIM for x in y]
        yc = [x - mu for x, mu in zip(y, mean)]
        var = [_exact_dot_r(x * x, head_ones) / RW_HEAD_DIM for x in yc]
        rk2 = [r_ref[rows, lanes_of(p)] * k_terms(rows, p)[2] * rk_ref[:, lanes_of(p)]
               for p in range(n_pairs)]
        bsum = [_exact_dot_r(x, head_ones) for x in rk2]
        for p in range(n_pairs):
            lanes = lanes_of(p)
            yn = yc[p] * lax.rsqrt(var[p] + RW_LN_EPS) * lnw_ref[:, lanes] + lnb_ref[:, lanes]
            out = (yn + bsum[p] * v_ref[rows, lanes]) * g_ref[rows, lanes]
            o_ref[rows, lanes] = out.astype(o_ref.dtype)
        return carry

    lax.fori_loop(0, n_chunks, finish, 0)


def _rwkv(batch, seq_len, r, k, v, lw, eta, g, k_k, k_a, r_k, ln_w, ln_b):
    n = r.shape[0]
    tb = TB_REC
    nt = seq_len // tb
    n_pairs = RW_WIDTH // LANES
    per_chunk = (tb // CHUNK, n_pairs)
    row = pl.BlockSpec((tb, RW_WIDTH), lambda b, t: (b * nt + t, 0))
    vecs = [p.reshape(1, RW_WIDTH) for p in (k_k, k_a, r_k, ln_w, ln_b)]
    return pl.pallas_call(
        _rwkv_kernel,
        grid=(batch, nt),
        in_specs=[row] * 6 + [_const_spec((1, RW_WIDTH))] * 5,
        out_specs=row,
        out_shape=jax.ShapeDtypeStruct((n, RW_WIDTH), BF16),
        scratch_shapes=[pltpu.VMEM((n_pairs, LANES, LANES), F32),
                        pltpu.VMEM(per_chunk + (2 * LANES, LANES), BF16),
                        pltpu.VMEM(per_chunk + (LANES, LANES), F32),
                        pltpu.VMEM(per_chunk + (LANES, LANES), F32),
                        pltpu.VMEM(per_chunk + (LANES, LANES), F32)],
        compiler_params=_params("arbitrary", "arbitrary"),
        name="rwkv7",
    )(r, k, v, lw, eta, g, *vecs)


def _gla_kernel(q_ref, k_ref, v_ref, gg_ref, la_ref, nrm_ref, o_ref,
                s_ref, qst_ref, intra_ref, kv_ref, gam_ref):
    t = pl.program_id(1)

    @pl.when(t == 0)
    def _():
        s_ref[...] = jnp.zeros_like(s_ref)

    c = CHUNK
    n_pairs = GLA_KW // LANES
    n_chunks = q_ref.shape[0] // c
    row = lax.broadcasted_iota(jnp.int32, (c, c), 0)
    col = lax.broadcasted_iota(jnp.int32, (c, c), 1)
    cum_mat = (col <= row).astype(BF16)
    _, causal = _tri_masks(2 * c, c)
    scale = GLA_DK ** -0.5

    probs = [(ci, p) for ci in range(n_chunks) for p in range(n_pairs)]
    rows_of = lambda ci: slice(ci * c, (ci + 1) * c)
    lanes_of = lambda p: slice(p * LANES, (p + 1) * LANES)
    zipped = lambda f, *ls: [f(*xs) for xs in zip(*ls)]
    cum = [_exact_dot(cum_mat, la_ref[rows_of(ci), lanes_of(p)]) for ci, p in probs]
    cum_end = [x[c - 1:c, :] for x in cum]
    q_st = [_stack_heads(q_ref[rows_of(ci), lanes_of(p)] * scale * jnp.exp(x))
            for (ci, p), x in zip(probs, cum)]
    k_st = [_stack_heads(k_ref[rows_of(ci), lanes_of(p)] * jnp.exp(-x))
            for (ci, p), x in zip(probs, cum)]
    kh_t = [_stack_heads(k_ref[rows_of(ci), lanes_of(p)] * jnp.exp(e - x)).T
            for (ci, p), x, e in zip(probs, cum, cum_end)]
    v_st = [jnp.concatenate([v_ref[rows_of(ci), (2 * p) * GLA_DV:(2 * p + 1) * GLA_DV],
                             v_ref[rows_of(ci), (2 * p + 1) * GLA_DV:(2 * p + 2) * GLA_DV]], axis=0)
            for ci, p in probs]
    scores = zipped(lambda a, b: jnp.where(causal, _dot_nt(a, b), 0.0), q_st, k_st)
    intra = zipped(_dot, scores, v_st)
    kv = zipped(_dot, kh_t, v_st)
    for i, (ci, p) in enumerate(probs):
        qst_ref[ci, p] = q_st[i].astype(qst_ref.dtype)
        intra_ref[ci, p] = intra[i]
        kv_ref[ci, p] = kv[i]
        gam_ref[ci, p] = jnp.exp(jnp.broadcast_to(cum_end[i], (LANES, LANES))).T

    def advance(ci, carry):
        rows = pl.ds(pl.multiple_of(ci * c, c), c)
        s0 = [s_ref[p] for p in range(n_pairs)]
        o_st = [intra_ref[ci, p] + _dot(qst_ref[ci, p], s0[p]) for p in range(n_pairs)]
        for p in range(n_pairs):
            s_ref[p] = gam_ref[ci, p] * s0[p] + kv_ref[ci, p]
            for hh in range(2):
                vl = slice((2 * p + hh) * GLA_DV, (2 * p + hh + 1) * GLA_DV)
                o = o_st[p][hh * c:(hh + 1) * c]
                o = o * lax.rsqrt(jnp.mean(o * o, axis=-1, keepdims=True) + GLA_NORM_EPS)
                o = o * nrm_ref[:, vl]
                gg = gg_ref[rows, vl].astype(F32)
                o_ref[rows, vl] = (o * (gg * _sigmoid(gg))).astype(o_ref.dtype)
        return carry

    lax.fori_loop(0, n_chunks, advance, 0)


def _gla(batch, seq_len, gq, gk, gv, gg, la, gla_norm):
    n = gq.shape[0]
    tb = TB_REC
    nt = seq_len // tb
    n_pairs = GLA_KW // LANES
    per_chunk = (tb // CHUNK, n_pairs)
    rowk = pl.BlockSpec((tb, GLA_KW), lambda b, t: (b * nt + t, 0))
    rowv = pl.BlockSpec((tb, GLA_VW), lambda b, t: (b * nt + t, 0))
    return pl.pallas_call(
        _gla_kernel,
        grid=(batch, nt),
        in_specs=[rowk, rowk, rowv, rowv, rowk, _const_spec((1, GLA_VW))],
        out_specs=rowv,
        out_shape=jax.ShapeDtypeStruct((n, GLA_VW), BF16),
        scratch_shapes=[pltpu.VMEM((n_pairs, LANES, GLA_DV), F32),
                        pltpu.VMEM(per_chunk + (LANES, LANES), BF16),
                        pltpu.VMEM(per_chunk + (LANES, GLA_DV), F32),
                        pltpu.VMEM(per_chunk + (LANES, GLA_DV), F32),
                        pltpu.VMEM(per_chunk + (LANES, GLA_DV), F32)],
        compiler_params=_params("arbitrary", "arbitrary"),
        name="gla",
    )(gq, gk, gv, gg, la, gla_norm.reshape(1, GLA_VW))


def _odd_in_kernel(x_ref, gn_ref, w_ref, q_ref, k_ref, vt_ref):
    h = _rms_norm(x_ref[...], gn_ref[...], NORM_EPS)
    qkv = _dot(h, w_ref[...])
    d = q_ref.shape[1]
    q_ref[...] = (qkv[:, :d] * (DA_QK_DIM ** -0.5 * LOG2E)).astype(q_ref.dtype)
    k_ref[...] = qkv[:, d:2 * d].astype(k_ref.dtype)
    tk = vt_ref.shape[3]
    for sub in range(vt_ref.shape[1]):
        vt_ref[0, sub] = qkv[sub * tk:(sub + 1) * tk, 2 * d:].T.astype(vt_ref.dtype)


def _odd_in(x2, seq_len, gn, w_qkv):
    n, d = x2.shape
    tm, tk = TM_PROJ, T_ATT
    nt, sub = seq_len // tm, tm // tk
    row = pl.BlockSpec((tm, d), lambda i: (i, 0))
    return pl.pallas_call(
        _odd_in_kernel,
        grid=(n // tm,),
        in_specs=[row, _const_spec((1, d)), _const_spec(w_qkv.shape)],
        out_specs=[row, row, pl.BlockSpec((1, sub, d, tk), lambda i: (i // nt, i % nt, 0, 0))],
        out_shape=[jax.ShapeDtypeStruct((n, d), BF16)] * 2
        + [jax.ShapeDtypeStruct((n // seq_len, seq_len // tk, d, tk), BF16)],
        compiler_params=_params("arbitrary"),
        name="odd_in",
    )(x2, gn.reshape(1, d), w_qkv.astype(BF16))


def _attn_kernel(lam_ref, q_ref, k_ref, vt_ref, bias_ref, subln_ref, o_ref,
                 m_ref, acc_ref, qst_ref, p_ref, alpha_ref, *, lam_init):
    i = pl.program_id(2)
    tq = q_ref.shape[0]
    tk = vt_ref.shape[3]
    ratio = tq // tk
    n_blocks = ratio * (i + 1)
    dv = vt_ref.shape[2]

    qst_ref[...] = _stack_heads(q_ref[...].astype(F32)).T.astype(BF16)
    m_ref[...] = jnp.full_like(m_ref, -jnp.inf)
    acc_ref[...] = jnp.zeros_like(acc_ref)
    ones_rows = jnp.ones((acc_ref.shape[0] - dv, tk), BF16)

    def accumulate(j, slot):
        lhs = jnp.concatenate([vt_ref[0, j], ones_rows], axis=0)
        acc_ref[...] = alpha_ref[slot] * acc_ref[...] + jnp.dot(
            lhs, p_ref[slot], preferred_element_type=F32)

    p_ref[1] = jnp.zeros(p_ref.shape[1:], p_ref.dtype)
    alpha_ref[1] = jnp.ones(alpha_ref.shape[1:], F32)

    def stage(j, cur, prev):
        k_blk = k_ref[pl.ds(pl.multiple_of(j * tk, tk), tk), :]
        tile = jnp.minimum(n_blocks - 1 - j, ratio + 1)
        for g in range(2 * tq // ATT_GROUP):
            cols = slice(g * ATT_GROUP, (g + 1) * ATT_GROUP)
            s = jnp.dot(k_blk, qst_ref[:, cols], preferred_element_type=F32)
            s = s + bias_ref[0, tile, :, cols]
            m_prev = m_ref[:, cols]
            m_new = jnp.maximum(m_prev, jnp.max(s, axis=0, keepdims=True))
            p_ref[cur, :, cols] = jnp.exp2(s - m_new).astype(p_ref.dtype)
            alpha_ref[cur, :, cols] = jnp.exp2(m_prev - m_new)
            m_ref[:, cols] = m_new
        accumulate(jnp.maximum(j - 1, 0), prev)

    def body(jj, carry):
        stage(2 * jj, 0, 1)
        stage(2 * jj + 1, 1, 0)
        return carry

    lax.fori_loop(0, n_blocks // 2, body, 0)
    accumulate(n_blocks - 1, 1)

    lam_v = lam_ref[...]
    lam = (jnp.exp(jnp.sum(lam_v[0:1] * lam_v[1:2], axis=1, keepdims=True))
           - jnp.exp(jnp.sum(lam_v[2:3] * lam_v[3:4], axis=1, keepdims=True)) + lam_init)
    o_all = acc_ref[:dv, :] * (1.0 / acc_ref[dv:dv + 1, :])
    o = o_all[:, :tq] - lam * o_all[:, tq:]
    o = o * lax.rsqrt(jnp.mean(o * o, axis=0, keepdims=True) + DA_SUBLN_EPS)
    o = o * subln_ref[...] * (1.0 - lam_init)
    o_ref[...] = o.T.astype(o_ref.dtype)


def _t5_causal_buckets(dist):
    max_exact = REL_BUCKETS // 2
    ratio = jnp.maximum(dist, max_exact).astype(F32) / max_exact
    large = max_exact + (jnp.log(ratio) / math.log(REL_MAX_DIST / max_exact)
                         * (REL_BUCKETS - max_exact)).astype(jnp.int32)
    large = jnp.minimum(large, REL_BUCKETS - 1)
    return jnp.where(dist < max_exact, dist, large)


def _attn_bias_tiles(rel_bias, tq, tk):
    heads = rel_bias.shape[1]
    ratio = tq // tk
    span = tq + tk
    by_dist = rel_bias[_t5_causal_buckets(jnp.arange(span + tk, dtype=jnp.int32))].T
    idx = np.arange(span)
    delta = np.where(idx < tq, idx, idx - span)

    def toeplitz(vec):
        flat = jnp.tile(vec, (1, tk))[:, :tk * (span - 1)]
        return flat.reshape(heads, tk, span - 1)[:, :, :tq]

    tiles = []
    for d in range(ratio + 1):
        dist = delta + tk * (d - ratio + 1)
        vals = by_dist[:, np.clip(dist, 0, None)]
        tiles.append(toeplitz(jnp.where(dist >= 0, vals, NEG_INF)))
    assert tk * 2 - (tk - 1) > REL_MAX_DIST
    tiles.append(jnp.broadcast_to(rel_bias[REL_BUCKETS - 1][:, None, None], tiles[0].shape))
    tiles = jnp.stack(tiles, axis=1)
    return (jnp.concatenate([tiles, tiles], axis=3) * LOG2E).astype(F32)


def _diff_attn(batch, seq_len, q, k, vt, lam_params, subln, bias_tiles, lam_init):
    n, d = q.shape
    heads = d // DA_V_DIM
    tq, tk = TQ_ATT, T_ATT
    nq = seq_len // tq
    assert (tq // tk) % 2 == 0
    qspec = pl.BlockSpec((tq, DA_V_DIM), lambda h, b, i: (b * nq + i, h))
    subln_cols = jnp.broadcast_to(subln.reshape(DA_V_DIM, 1), (DA_V_DIM, tq))
    return pl.pallas_call(
        functools.partial(_attn_kernel, lam_init=lam_init),
        grid=(heads, batch, nq),
        in_specs=[_const_spec(lam_params.shape), qspec,
                  pl.BlockSpec((seq_len, DA_V_DIM), lambda h, b, i: (b, h)),
                  pl.BlockSpec((1, seq_len // tk, DA_V_DIM, tk), lambda h, b, i: (b, 0, h, 0)),
                  pl.BlockSpec((1,) + bias_tiles.shape[1:], lambda h, b, i: (h, 0, 0, 0)),
                  _const_spec((DA_V_DIM, tq))],
        out_specs=qspec,
        out_shape=jax.ShapeDtypeStruct((n, d), BF16),
        scratch_shapes=[pltpu.VMEM((1, 2 * tq), F32),
                        pltpu.VMEM((DA_V_DIM + BF16_ROWS, 2 * tq), F32),
                        pltpu.VMEM((DA_V_DIM, 2 * tq), BF16), pltpu.VMEM((2, tk, 2 * tq), BF16),
                        pltpu.VMEM((2, 1, 2 * tq), F32)],
        compiler_params=_params("arbitrary", "arbitrary", "arbitrary"),
        name="diff_attn",
    )(lam_params, q, k, vt, bias_tiles, subln_cols)


def _ffn_kernel(*refs, n_mix, final):
    x_ref = refs[0]
    mix_refs = refs[1:1 + n_mix]
    wout_ref, gn_ref, wg_ref, wu_ref, wd_ref = refs[1 + n_mix:6 + n_mix]
    gfin_ref = refs[6 + n_mix] if final else None
    o_ref = refs[-1]

    mix = jnp.concatenate([m_ref[...] for m_ref in mix_refs], axis=1)
    x = x_ref[...] + jnp.dot(mix, wout_ref[...], preferred_element_type=F32)
    h = _rms_norm(x, gn_ref[...], NORM_EPS).astype(BF16)
    acc = x
    hidden = wg_ref.shape[1]
    for c0 in range(0, hidden, FFN_HC):
        cols = slice(c0, c0 + FFN_HC)
        gate = jnp.dot(h, wg_ref[:, cols], preferred_element_type=F32)
        up = jnp.dot(h, wu_ref[:, cols], preferred_element_type=F32)
        act = (gate * _sigmoid(gate)) * up
        acc = acc + jnp.dot(act.astype(BF16), wd_ref[cols, :], preferred_element_type=F32)
    if final:
        acc = _rms_norm(acc, gfin_ref[...], NORM_EPS)
    o_ref[...] = acc


def _ffn(x2, mixes, w_out, gn, wg, wu, wd, g_final=None):
    n, d = x2.shape
    tm = TM_FFN
    final = g_final is not None
    row = lambda w: pl.BlockSpec((tm, w), lambda i: (i, 0))
    consts = [w_out.astype(BF16), gn.reshape(1, d), wg.astype(BF16), wu.astype(BF16),
              wd.astype(BF16)]
    if final:
        consts.append(g_final.reshape(1, d))
    return pl.pallas_call(
        functools.partial(_ffn_kernel, n_mix=len(mixes), final=final),
        grid=(n // tm,),
        in_specs=([row(d)] + [row(m.shape[1]) for m in mixes]
                  + [_const_spec(c.shape) for c in consts]),
        out_specs=row(d),
        out_shape=jax.ShapeDtypeStruct((n, d), F32),
        compiler_params=_params("arbitrary"),
        name="ffn",
    )(x2, *mixes, *consts)


def _diff_lambda_init(layer):
    return 0.8 - 0.6 * math.exp(-0.3 * layer)


def kernel(x, rel_bias, norm_mix, norm_ffn, norm_final, ab_w_in, ab_w_out, rw_mu_rkv, rw_mu_wag, rw_w0, rw_w1, rw_w2, rw_a0, rw_a1, rw_a2, rw_g1, rw_g2, rw_k_k, rw_k_a, rw_r_k, rw_ln_w, rw_ln_b, gla_wa1, gla_wa2, gla_ba, gla_norm, da_w_qkv, da_w_out, da_lam_q1, da_lam_k1, da_lam_q2, da_lam_k2, da_subln, ffn_w_gate, ffn_w_up, ffn_w_down):
    batch, seq_len, d = x.shape
    depth = norm_mix.shape[0]
    x2 = x.reshape(batch * seq_len, d)
    bias_tiles = _attn_bias_tiles(rel_bias, TQ_ATT, T_ATT)
    for layer in range(depth):
        i = layer // 2
        if layer % 2 == 0:
            (r, k, v, lw, eta, g, gq, gk, gv, gg, la) = _even_in(
                x2, seq_len, norm_mix[layer], ab_w_in[i], rw_mu_rkv[i], rw_mu_wag[i],
                rw_w0[i], rw_w1[i], rw_w2[i], rw_a0[i], rw_a1[i], rw_a2[i], rw_g1[i], rw_g2[i],
                gla_wa1[i], gla_wa2[i], gla_ba[i])
            o_a = _rwkv(batch, seq_len, r, k, v, lw, eta, g, rw_k_k[i], rw_k_a[i],
                        rw_r_k[i], rw_ln_w[i], rw_ln_b[i])
            o_b = _gla(batch, seq_len, gq, gk, gv, gg, la, gla_norm[i])
            mixes = [o_a, o_b]
            w_out = ab_w_out[i]
        else:
            q, k, vt = _odd_in(x2, seq_len, norm_mix[layer], da_w_qkv[i])
            lam_params = jnp.stack([da_lam_q1[i], da_lam_k1[i], da_lam_q2[i], da_lam_k2[i]])
            o = _diff_attn(batch, seq_len, q, k, vt, lam_params, da_subln[i], bias_tiles,
                           _diff_lambda_init(layer))
            mixes = [o]
            w_out = da_w_out[i]
        x2 = _ffn(x2, mixes, w_out, norm_ffn[layer], ffn_w_gate[layer], ffn_w_up[layer],
                  ffn_w_down[layer], norm_final if layer == depth - 1 else None)
    return x2.reshape(batch, seq_len, d)
```
